```python
import math
import jax, jax.numpy as jnp
from jax import lax
import numpy as np

D_MODEL = 1024
BATCH = 4
SEQ = 4096
DEPTH = 4
DEC_BATCH = 128
DEC_SEQ = 8
PAST_LEN = 2048
PAGE_SIZE = 128

SB_HEADS = 4
SB_HEAD_DIM = 64
DIFF_HEADS = 4
DIFF_HEAD_DIM = 64
MOBA_HEADS = 4
MOBA_HEAD_DIM = 64
MOBA_BLOCK = 256
MOBA_TOPK = 3
N_BRANCHES = 3
D_FF = 4 * D_MODEL
ROPE_THETA = 10000.0
NORM_EPS = 1e-6
QUERY_BLOCK = 128
SB_W = SB_HEADS * SB_HEAD_DIM
DIFF_QK_W = DIFF_HEADS * 2 * DIFF_HEAD_DIM
DIFF_V_W = DIFF_HEADS * 2 * DIFF_HEAD_DIM
MOBA_W = MOBA_HEADS * MOBA_HEAD_DIM
IN_W = 3 * SB_W + 2 * DIFF_QK_W + DIFF_V_W + 3 * MOBA_W + N_BRANCHES * D_MODEL

kernel_name = 'hybrid_sb_diff_moba_decoder_step'


def lambda_init(layer):
    return 0.8 - 0.6 * math.exp(-0.3 * layer)


def rms_norm(x, g):
    xf = x.astype(jnp.float32)
    y = xf * lax.rsqrt(jnp.mean(xf * xf, axis=-1, keepdims=True) + NORM_EPS)
    return (y * g.astype(jnp.float32)).astype(x.dtype)


def rope(x, pos):
    half = x.shape[-1] // 2
    inv_freq = ROPE_THETA ** (-jnp.arange(half, dtype=jnp.float32) / half)
    ang = pos.astype(jnp.float32)[:, None] * inv_freq[None, :]
    bshape = (1, x.shape[1]) + (1,) * (x.ndim - 3) + (half,)
    cos = jnp.cos(ang).reshape(bshape)
    sin = jnp.sin(ang).reshape(bshape)
    xf = x.astype(jnp.float32)
    x1, x2 = xf[..., :half], xf[..., half:]
    return jnp.concatenate([x1 * cos - x2 * sin, x2 * cos + x1 * sin], axis=-1).astype(x.dtype)


def _block_size(tq, pref):
    return pref if tq % pref == 0 else tq


def sweep_queries(fn, q, q_pos, block):
    b, tq = q.shape[:2]
    nb = tq // block
    qb = jnp.moveaxis(q.reshape((b, nb, block) + q.shape[2:]), 1, 0)
    pb = q_pos.reshape(nb, block)
    out = lax.map(lambda args: fn(args[0], args[1]), (qb, pb))
    out = jnp.moveaxis(out, 0, 1)
    return out.reshape((b, tq) + out.shape[3:])


def stick_breaking_attention(q, q_pos, k, v):
    scale = SB_HEAD_DIM ** -0.5
    k_pos = jnp.arange(k.shape[1], dtype=jnp.int32)

    def block(qc, pc):
        z = jnp.einsum('bqhd,blhd->bhql', qc, k).astype(jnp.float32) * scale
        past = k_pos[None, :] < pc[:, None]
        log_rem = jnp.where(past, jax.nn.log_sigmoid(-z), 0.0)
        after = lax.cumsum(log_rem, axis=3, reverse=True) - log_rem
        a = jnp.where(past, jnp.exp(jax.nn.log_sigmoid(z) + after), 0.0)
        return jnp.einsum('bhql,blhd->bqhd', a.astype(v.dtype), v)

    return sweep_queries(block, q, q_pos, _block_size(q.shape[1], QUERY_BLOCK))


def differential_attention(q, q_pos, k, v, lam):
    scale = DIFF_HEAD_DIM ** -0.5
    k_pos = jnp.arange(k.shape[1], dtype=jnp.int32)

    def block(qc, pc):
        s = jnp.einsum('bqhcd,blhcd->bhcql', qc, k).astype(jnp.float32) * scale
        causal = k_pos[None, :] <= pc[:, None]
        p = jax.nn.softmax(jnp.where(causal, s, -jnp.inf), axis=-1)
        w = p[:, :, 0] - lam * p[:, :, 1]
        return jnp.einsum('bhql,blhe->bqhe', w.astype(v.dtype), v)

    return sweep_queries(block, q, q_pos, _block_size(q.shape[1], QUERY_BLOCK))


def moba_attention(q, q_pos, k, v, q_block):
    b, l, h, d = k.shape
    n_blk = -(-l // MOBA_BLOCK)
    pad = n_blk * MOBA_BLOCK - l

    def to_blocks(a):
        a = jnp.pad(a, ((0, 0), (0, pad), (0, 0), (0, 0)))
        return a.reshape(b, n_blk, MOBA_BLOCK, h, d).transpose(0, 3, 1, 2, 4)

    kb, vb = to_blocks(k), to_blocks(v)
    k_mean = jnp.mean(kb.astype(jnp.float32), axis=3)
    n_sel = min(MOBA_TOPK, n_blk)
    scale = MOBA_HEAD_DIM ** -0.5
    bi = jnp.arange(b)[:, None, None, None]
    hi = jnp.arange(h)[None, :, None, None]
    blk_ids = jnp.arange(n_blk, dtype=jnp.int32)
    offs = jnp.arange(MOBA_BLOCK, dtype=jnp.int32)

    def block(qc, pc):
        nq = qc.shape[1]
        cur = pc // MOBA_BLOCK
        gate = jnp.einsum('bqhd,bhnd->bhqn', qc.astype(jnp.float32), k_mean)
        gate = jnp.where(blk_ids[None, :] < cur[:, None], gate, -jnp.inf)
        top_val, top_idx = lax.top_k(gate, n_sel)
        idx = jnp.concatenate([top_idx.astype(jnp.int32), jnp.broadcast_to(cur[None, None, :, None], (b, h, nq, 1))], axis=-1)
        valid = jnp.concatenate([jnp.isfinite(top_val), jnp.ones((b, h, nq, 1), dtype=bool)], axis=-1)
        kg = kb[bi, hi, idx]
        vg = vb[bi, hi, idx]
        s = jnp.einsum('bqhd,bhqsjd->bhqsj', qc, kg).astype(jnp.float32) * scale
        key_pos = idx[..., None] * MOBA_BLOCK + offs
        ok = valid[..., None] & (key_pos <= pc[None, None, :, None, None])
        s = jnp.where(ok, s, -jnp.inf).reshape(b, h, nq, -1)
        p = jax.nn.softmax(s, axis=-1).reshape(b, h, nq, n_sel + 1, MOBA_BLOCK)
        return jnp.einsum('bhqsj,bhqsjd->bqhd', p.astype(vg.dtype), vg)

    return sweep_queries(block, q, q_pos, q_block)


def split_projection(proj):
    widths = (SB_W, SB_W, SB_W, DIFF_QK_W, DIFF_QK_W, DIFF_V_W, MOBA_W, MOBA_W, MOBA_W, N_BRANCHES * D_MODEL)
    cuts = [int(c) for c in np.cumsum(widths)[:-1]]
    return jnp.split(proj, cuts, axis=-1)


def gather_pages(pool, page_table):
    rows = pool[page_table]
    return rows.reshape((page_table.shape[0], -1) + pool.shape[2:])


def token_mixers(xn, pos, past, layer, w_in, b_gate, diff_lambda, diff_subln_g, w_br_sb, w_br_diff, w_br_moba, w_out, moba_qb):
    b, t, _ = xn.shape
    proj = jnp.einsum('btd,de->bte', xn, w_in)
    q_sb, k_sb, v_sb, q_df, k_df, v_df, q_mb, k_mb, v_mb, gates = split_projection(proj)
    q_sb = q_sb.reshape(b, t, SB_HEADS, SB_HEAD_DIM)
    kv_sb = jnp.stack([k_sb.reshape(b, t, SB_HEADS, SB_HEAD_DIM), v_sb.reshape(b, t, SB_HEADS, SB_HEAD_DIM)], axis=2)
    q_df = rope(q_df.reshape(b, t, DIFF_HEADS, 2, DIFF_HEAD_DIM), pos)
    k_df = rope(k_df.reshape(b, t, DIFF_HEADS, 2, DIFF_HEAD_DIM), pos)
    kv_df = jnp.stack([k_df.reshape(b, t, DIFF_HEADS, 2 * DIFF_HEAD_DIM), v_df.reshape(b, t, DIFF_HEADS, 2 * DIFF_HEAD_DIM)], axis=2)
    q_mb = rope(q_mb.reshape(b, t, MOBA_HEADS, MOBA_HEAD_DIM), pos)
    kv_mb = jnp.stack([rope(k_mb.reshape(b, t, MOBA_HEADS, MOBA_HEAD_DIM), pos), v_mb.reshape(b, t, MOBA_HEADS, MOBA_HEAD_DIM)], axis=2)
    if past is None:
        full_sb, full_df, full_mb = kv_sb, kv_df, kv_mb
    else:
        full_sb = jnp.concatenate([past[0], kv_sb], axis=1)
        full_df = jnp.concatenate([past[1], kv_df], axis=1)
        full_mb = jnp.concatenate([past[2], kv_mb], axis=1)
    o_sb = stick_breaking_attention(q_sb, pos, full_sb[:, :, 0], full_sb[:, :, 1])
    lam_init = lambda_init(layer)
    lp = diff_lambda.astype(jnp.float32)
    lam = jnp.exp(jnp.sum(lp[0] * lp[1])) - jnp.exp(jnp.sum(lp[2] * lp[3])) + lam_init
    k_full_df = full_df[:, :, 0].reshape(b, full_df.shape[1], DIFF_HEADS, 2, DIFF_HEAD_DIM)
    o_df = differential_attention(q_df, pos, k_full_df, full_df[:, :, 1], lam)
    o_df = rms_norm(o_df, diff_subln_g) * (1.0 - lam_init)
    o_mb = moba_attention(q_mb, pos, full_mb[:, :, 0], full_mb[:, :, 1], moba_qb)
    y_sb = o_sb.reshape(b, t, SB_W) @ w_br_sb
    y_df = o_df.reshape(b, t, DIFF_V_W) @ w_br_diff
    y_mb = o_mb.reshape(b, t, MOBA_W) @ w_br_moba
    g = jax.nn.sigmoid((gates.reshape(b, t, N_BRANCHES, D_MODEL) + b_gate).astype(jnp.float32)).astype(xn.dtype)
    merged = g[:, :, 0] * y_sb + g[:, :, 1] * y_df + g[:, :, 2] * y_mb
    return merged @ w_out, kv_sb, kv_df, kv_mb


def run_trunk(x, pos, caches, page_table, moba_qb, g_mix, w_in, b_gate, diff_lambda, diff_subln_g, w_br_sb, w_br_diff, w_br_moba, w_out, g_mlp, w_up, w_down, g_final):
    new_sb, new_df, new_mb = [], [], []
    for layer in range(DEPTH):
        past = None if caches is None else (gather_pages(caches[0][layer], page_table), gather_pages(caches[1][layer], page_table), gather_pages(caches[2][layer], page_table))
        mix, kv_sb, kv_df, kv_mb = token_mixers(rms_norm(x, g_mix[layer]), pos, past, layer, w_in[layer], b_gate[layer], diff_lambda[layer], diff_subln_g[layer], w_br_sb[layer], w_br_diff[layer], w_br_moba[layer], w_out[layer], moba_qb)
        x = x + mix
        h = rms_norm(x, g_mlp[layer])
        x = x + jnp.square(jax.nn.relu(h @ w_up[layer])) @ w_down[layer]
        new_sb.append(kv_sb)
        new_df.append(kv_df)
        new_mb.append(kv_mb)
    return rms_norm(x, g_final), jnp.stack(new_sb), jnp.stack(new_df), jnp.stack(new_mb)


def setup_inputs(seed: int = 0) -> dict:
    key = jax.random.key(seed)
    ks = jax.random.split(key, 24)
    f32 = jnp.float32
    n_pages = PAST_LEN // PAGE_SIZE
    n_used = DEC_BATCH * n_pages
    n_phys = n_used + n_used // 4

    def nrm(k, shape, scale):
        return jax.random.normal(k, shape, f32) * scale

    page_table = jax.random.permutation(ks[5], n_phys)[:n_used].reshape(DEC_BATCH, n_pages).astype(jnp.int32)
    return {
        'x_prompt': nrm(ks[0], (BATCH, SEQ, D_MODEL), 1.0),
        'x_sample': nrm(ks[1], (DEC_BATCH, DEC_SEQ, D_MODEL), 1.0),
        'cache_sb_kv': nrm(ks[2], (DEPTH, n_phys, PAGE_SIZE, 2, SB_HEADS, SB_HEAD_DIM), 1.0),
        'cache_diff_kv': nrm(ks[3], (DEPTH, n_phys, PAGE_SIZE, 2, DIFF_HEADS, 2 * DIFF_HEAD_DIM), 1.0),
        'cache_moba_kv': nrm(ks[4], (DEPTH, n_phys, PAGE_SIZE, 2, MOBA_HEADS, MOBA_HEAD_DIM), 1.0),
        'page_table': page_table,
        'g_mix': 1.0 + nrm(ks[6], (DEPTH, D_MODEL), 0.02),
        'w_in': nrm(ks[7], (DEPTH, D_MODEL, IN_W), D_MODEL ** -0.5),
        'b_gate': nrm(ks[8], (DEPTH, N_BRANCHES, D_MODEL), 0.1),
        'diff_lambda': nrm(ks[9], (DEPTH, 4, DIFF_HEAD_DIM), 0.1),
        'diff_subln_g': 1.0 + nrm(ks[10], (DEPTH, 2 * DIFF_HEAD_DIM), 0.02),
        'w_br_sb': nrm(ks[11], (DEPTH, SB_W, D_MODEL), SB_W ** -0.5),
        'w_br_diff': nrm(ks[12], (DEPTH, DIFF_V_W, D_MODEL), DIFF_V_W ** -0.5),
        'w_br_moba': nrm(ks[13], (DEPTH, MOBA_W, D_MODEL), MOBA_W ** -0.5),
        'w_out': nrm(ks[14], (DEPTH, D_MODEL, D_MODEL), D_MODEL ** -0.5),
        'g_mlp': 1.0 + nrm(ks[15], (DEPTH, D_MODEL), 0.02),
        'w_up': nrm(ks[16], (DEPTH, D_MODEL, D_FF), D_MODEL ** -0.5),
        'w_down': nrm(ks[17], (DEPTH, D_FF, D_MODEL), D_FF ** -0.5),
        'g_final': 1.0 + nrm(ks[18], (D_MODEL,), 0.02),
    }


def reference(x_prompt, x_sample, cache_sb_kv, cache_diff_kv, cache_moba_kv, page_table, g_mix, w_in, b_gate, diff_lambda, diff_subln_g, w_br_sb, w_br_diff, w_br_moba, w_out, g_mlp, w_up, w_down, g_final):
    past_len = page_table.shape[1] * cache_sb_kv.shape[2]
    pos_p = jnp.arange(x_prompt.shape[1], dtype=jnp.int32)
    pos_s = past_len + jnp.arange(x_sample.shape[1], dtype=jnp.int32)
    y_prompt, sb_kv_prompt, diff_kv_prompt, moba_kv_prompt = run_trunk(
        x_prompt, pos_p, None, None, _block_size(x_prompt.shape[1], QUERY_BLOCK),
        g_mix, w_in, b_gate, diff_lambda, diff_subln_g, w_br_sb, w_br_diff, w_br_moba, w_out, g_mlp, w_up, w_down, g_final)
    y_sample, sb_kv_sample, diff_kv_sample, moba_kv_sample = run_trunk(
        x_sample, pos_s, (cache_sb_kv, cache_diff_kv, cache_moba_kv), page_table, 1,
        g_mix, w_in, b_gate, diff_lambda, diff_subln_g, w_br_sb, w_br_diff, w_br_moba, w_out, g_mlp, w_up, w_down, g_final)
    return (y_prompt, y_sample, sb_kv_prompt, sb_kv_sample, diff_kv_prompt, diff_kv_sample, moba_kv_prompt, moba_kv_sample)
```

```python
import functools
import math

import jax
import jax.numpy as jnp
from jax import lax
from jax.experimental import pallas as pl
from jax.experimental.pallas import tpu as pltpu

F32 = jnp.float32
BF16 = jnp.bfloat16

HEAD_DIM = 64
ROPE_THETA = 10000.0
NORM_EPS = 1e-6
MOBA_BLOCK = 256
MOBA_TOPK = 3
N_BRANCHES = 3

LANES = 128
VMEM_LIMIT_BYTES = 56 * 1024 * 1024

PROMPT_ROW_TILE = 256
ATTN_TILE = 256
MLP_ROW_TILE = 512
MLP_FF_TILE = 1024
DEC_CHUNK = 256


def _lambda_init(layer):
    return 0.8 - 0.6 * math.exp(-0.3 * layer)


def _params(*semantics):
    return pltpu.CompilerParams(dimension_semantics=semantics, vmem_limit_bytes=VMEM_LIMIT_BYTES)


def _dot(a, b):
    return jnp.dot(a, b, preferred_element_type=F32)


def _dot_nt(a, b):
    return lax.dot_general(a, b, (((1,), (1,)), ((), ())), preferred_element_type=F32)


def _rms(x, g):
    return (x * lax.rsqrt(jnp.mean(x * x, axis=-1, keepdims=True) + NORM_EPS)) * g


def _softplus(z):
    return jnp.maximum(z, 0.0) + jnp.log1p(jnp.exp(-jnp.abs(z)))


def _strict_upper(n):
    r = lax.broadcasted_iota(jnp.int32, (n, n), 0)
    c = lax.broadcasted_iota(jnp.int32, (n, n), 1)
    return (r > c).astype(BF16)


def _rev_excl_cumsum(x, upper):
    hi = x.astype(BF16)
    lo = (x - hi.astype(F32)).astype(BF16)
    return _dot(hi, upper) + _dot(lo, upper)


def _rope_tables(pos):
    half = HEAD_DIM // 2
    inv_freq = ROPE_THETA ** (-jnp.arange(half, dtype=F32) / half)
    ang = pos.astype(F32)[:, None] * inv_freq[None, :]
    cos, sin = jnp.cos(ang), jnp.sin(ang)
    reps = LANES // HEAD_DIM
    return (jnp.tile(jnp.concatenate([cos, cos], -1), (1, reps)),
            jnp.tile(jnp.concatenate([-sin, sin], -1), (1, reps)), cos.T, sin.T)


def _rope(v, cos, sin_signed):
    half = HEAD_DIM // 2
    first = (lax.broadcasted_iota(jnp.int32, (v.shape[0], LANES), 1) % HEAD_DIM) < half
    outs = []
    for c in range(v.shape[1] // LANES):
        s = v[:, c * LANES:(c + 1) * LANES]
        swapped = jnp.where(first, pltpu.roll(s, LANES - half, 1), pltpu.roll(s, half, 1))
        outs.append(s * cos + swapped * sin_signed)
    return outs[0] if len(outs) == 1 else jnp.concatenate(outs, axis=-1)


def _rope_feature_major(vt, cos_t, sin_t):
    half = HEAD_DIM // 2
    outs = []
    for h in range(vt.shape[0] // HEAD_DIM):
        x1 = vt[h * HEAD_DIM:h * HEAD_DIM + half]
        x2 = vt[h * HEAD_DIM + half:(h + 1) * HEAD_DIM]
        outs += [x1 * cos_t - x2 * sin_t, x2 * cos_t + x1 * sin_t]
    return jnp.concatenate(outs, axis=0)


def _proj_kernel(x_ref, g_ref, w_ref, wt_ref, cos_ref, sin_ref, cost_ref, sint_ref,
                 qsb_ref, kvsb_ref, qdf_ref, kvdf_ref, qmb_ref, kvmb_ref, gates_ref, kmean_ref,
                 *, sb_w, df_w, mb_w, gate_w):
    xb = _rms(x_ref[...], g_ref[...]).astype(BF16)
    cos, sin = cos_ref[...], sin_ref[...]
    scale = HEAD_DIM ** -0.5

    def mm(lo, width):
        return _dot(xb, w_ref[:, lo:lo + width])

    o = 0
    qsb_ref[...] = mm(o, sb_w) * scale
    o += sb_w
    qdf_ref[...] = _rope(mm(o, df_w), cos, sin) * scale
    o += df_w
    k_df = _rope(mm(o, df_w), cos, sin)
    o += df_w
    v_df = mm(o, df_w)
    o += df_w
    groups = df_w // LANES
    for j in range(groups):
        kvdf_ref[:, j, :] = k_df[:, j * LANES:(j + 1) * LANES]
        kvdf_ref[:, groups + j, :] = v_df[:, j * LANES:(j + 1) * LANES]
    qmb_ref[...] = _rope(mm(o, mb_w), cos, sin) * scale
    o += mb_w
    chunk = 1024
    for c in range(0, gate_w, chunk):
        gates_ref[:, c:c + chunk] = mm(o + c, chunk)

    kv_t = _dot_nt(wt_ref[...], xb)
    kvsb_ref[...] = kv_t[:2 * sb_w]
    k_mb = _rope_feature_major(kv_t[2 * sb_w:2 * sb_w + mb_w], cost_ref[...], sint_ref[...])
    kvmb_ref[:mb_w] = k_mb
    kvmb_ref[mb_w:] = kv_t[2 * sb_w + mb_w:]
    kmean_ref[...] = jnp.mean(k_mb, axis=-1, keepdims=True)


def _in_projection(x, g, w_tok, w_feat, layer, tables, widths, tm, seq_tiles):
    t, d = x.shape
    sb_w, df_w, mb_w, gate_w = widths
    cos, sin, cos_t, sin_t = tables
    tab_tiles = cos.shape[0] // tm
    n_tiles = t // tm
    half = HEAD_DIM // 2

    def rows(w):
        return pl.BlockSpec((tm, w), lambda i: (i, 0))

    def feat(w):
        return pl.BlockSpec((None, w, tm), lambda i: (i // seq_tiles, 0, i % seq_tiles))

    kv_groups = 2 * df_w // LANES
    out_shapes = [
        jax.ShapeDtypeStruct((t, sb_w), F32),
        jax.ShapeDtypeStruct((n_tiles // seq_tiles, 2 * sb_w, seq_tiles * tm), F32),
        jax.ShapeDtypeStruct((t, df_w), F32),
        jax.ShapeDtypeStruct((t, kv_groups, LANES), F32),
        jax.ShapeDtypeStruct((t, mb_w), F32),
        jax.ShapeDtypeStruct((n_tiles // seq_tiles, 2 * mb_w, seq_tiles * tm), F32),
        jax.ShapeDtypeStruct((t, gate_w), F32),
        jax.ShapeDtypeStruct((n_tiles, mb_w, 1), F32),
    ]
    out_specs = [rows(sb_w), feat(2 * sb_w), rows(df_w),
                 pl.BlockSpec((tm, kv_groups, LANES), lambda i: (i, 0, 0)),
                 rows(mb_w), feat(2 * mb_w), rows(gate_w),
                 pl.BlockSpec((None, mb_w, 1), lambda i: (i, 0, 0))]
    return pl.pallas_call(
        functools.partial(_proj_kernel, sb_w=sb_w, df_w=df_w, mb_w=mb_w, gate_w=gate_w),
        grid=(n_tiles,),
        in_specs=[rows(d),
                  pl.BlockSpec((None, 1, d), lambda i: (layer, 0, 0)),
                  pl.BlockSpec((None, d, w_tok.shape[-1]), lambda i: (layer, 0, 0)),
                  pl.BlockSpec((None, w_feat.shape[1], d), lambda i: (layer, 0, 0)),
                  pl.BlockSpec((tm, LANES), lambda i: (i % tab_tiles, 0)),
                  pl.BlockSpec((tm, LANES), lambda i: (i % tab_tiles, 0)),
                  pl.BlockSpec((half, tm), lambda i: (0, i % tab_tiles)),
                  pl.BlockSpec((half, tm), lambda i: (0, i % tab_tiles))],
        out_specs=out_specs,
        out_shape=out_shapes,
        compiler_params=_params("parallel"),
        name="in_projection",
    )(x, g, w_tok, w_feat, cos, sin, cos_t, sin_t)


def _causal_pairs(n):
    qi = [i for i in range(n) for _ in range(i + 1)]
    kj = [i - j for i in range(n) for j in range(i + 1)]
    return jnp.asarray(qi, jnp.int32), jnp.asarray(kj, jnp.int32)


def _tile_mask(t, i, j, strict):
    qpos = i * t + lax.broadcasted_iota(jnp.int32, (t, t), 0)
    kpos = j * t + lax.broadcasted_iota(jnp.int32, (t, t), 1)
    return (kpos < qpos) if strict else (kpos <= qpos)


def _sb_prompt_kernel(qi_ref, kj_ref, q_ref, kt_ref, vt_ref, o_ref, acc_ref, carry_ref, *, heads):
    s = pl.program_id(1)
    i, j = qi_ref[s], kj_ref[s]
    t = q_ref.shape[0]

    @pl.when(j == i)
    def _():
        acc_ref[...] = jnp.zeros_like(acc_ref)
        carry_ref[...] = jnp.zeros_like(carry_ref)

    past = _tile_mask(t, i, j, True)
    upper = _strict_upper(t)
    for h in range(heads):
        cols = slice(h * HEAD_DIM, (h + 1) * HEAD_DIM)
        z = _dot(q_ref[:, cols].astype(BF16), kt_ref[cols, :].astype(BF16))
        sp = _softplus(z)
        log_rem = jnp.where(past, -sp, 0.0)
        after = _rev_excl_cumsum(log_rem, upper) + carry_ref[h]
        a = jnp.where(past, jnp.exp(z - sp + after), 0.0)
        acc_ref[:, cols] += _dot_nt(a.astype(BF16), vt_ref[cols, :].astype(BF16))
        carry_ref[h] += jnp.sum(log_rem, axis=-1, keepdims=True)

    @pl.when(j == 0)
    def _():
        o_ref[...] = acc_ref[...]


def _online_softmax_step(s, m_ref, l_ref, idx):
    m_old = m_ref[idx]
    m_new = jnp.maximum(m_old, jnp.max(s, axis=-1, keepdims=True))
    alpha = jnp.exp(m_old - m_new)
    p = jnp.exp(s - m_new)
    l_ref[idx] = alpha * l_ref[idx] + jnp.sum(p, axis=-1, keepdims=True)
    m_ref[idx] = m_new
    return alpha, p


def _diff_lambda(lp, lam_init):
    a = jnp.sum(lp[0:1] * lp[1:2], axis=-1, keepdims=True)
    b = jnp.sum(lp[2:3] * lp[3:4], axis=-1, keepdims=True)
    return jnp.exp(a) - jnp.exp(b) + lam_init


def _diff_prompt_kernel(qi_ref, kj_ref, q_ref, kv_ref, lp_ref, g_ref, o_ref, acc_ref, m_ref, l_ref,
                        *, heads, lam_init):
    s = pl.program_id(1)
    i, j = qi_ref[s], kj_ref[s]
    t = q_ref.shape[0]
    vw = 2 * HEAD_DIM

    @pl.when(j == i)
    def _():
        acc_ref[...] = jnp.zeros_like(acc_ref)
        m_ref[...] = jnp.full_like(m_ref, -jnp.inf)
        l_ref[...] = jnp.zeros_like(l_ref)

    causal = _tile_mask(t, i, j, False)
    for h in range(heads):
        kb = kv_ref[:, h, :].astype(BF16)
        vb = kv_ref[:, heads + h, :].astype(BF16)
        for c in range(2):
            sc = _dot_nt(q_ref[:, h * vw + c * HEAD_DIM:h * vw + (c + 1) * HEAD_DIM].astype(BF16),
                         kb[:, c * HEAD_DIM:(c + 1) * HEAD_DIM])
            sc = jnp.where(causal, sc, -jnp.inf)
            alpha, p = _online_softmax_step(sc, m_ref, l_ref, 2 * h + c)
            acc_ref[c, :, h * vw:(h + 1) * vw] = alpha * acc_ref[c, :, h * vw:(h + 1) * vw] + _dot(p.astype(BF16), vb)

    @pl.when(j == 0)
    def _():
        lam = _diff_lambda(lp_ref[...], lam_init)
        for h in range(heads):
            o1 = acc_ref[0, :, h * vw:(h + 1) * vw] / l_ref[2 * h]
            o2 = acc_ref[1, :, h * vw:(h + 1) * vw] / l_ref[2 * h + 1]
            o_ref[:, h * vw:(h + 1) * vw] = _rms(o1 - lam * o2, g_ref[...]) * (1.0 - lam_init)


def _topk_select(gates, valid, k):
    sel = []
    for n in range(len(gates)):
        rank = jnp.zeros_like(gates[n])
        for m in range(len(gates)):
            if m == n:
                continue
            beats = (gates[m] >= gates[n]) if m < n else (gates[m] > gates[n])
            if valid is not None:
                beats = jnp.logical_and(valid[m], beats)
            rank = rank + jnp.where(beats, 1.0, 0.0)
        top = rank < k
        sel.append(top if valid is None else jnp.logical_and(valid[n], top))
    return sel


def _moba_prompt_kernel(qi_ref, kj_ref, q_ref, kt_ref, vt_ref, kmean_ref, o_ref, acc_ref, m_ref, l_ref, sel_ref,
                        *, heads):
    s = pl.program_id(1)
    i, j = qi_ref[s], kj_ref[s]
    t = q_ref.shape[0]
    n_blk = kmean_ref.shape[0]

    @pl.when(j == i)
    def _():
        acc_ref[...] = jnp.zeros_like(acc_ref)
        m_ref[...] = jnp.full_like(m_ref, -jnp.inf)
        l_ref[...] = jnp.zeros_like(l_ref)
        blk = lax.broadcasted_iota(jnp.int32, (t, n_blk), 1)
        for h in range(heads):
            cols = slice(h * HEAD_DIM, (h + 1) * HEAD_DIM)
            qh = q_ref[:, cols]
            gates = [jnp.sum(qh * kmean_ref[n:n + 1, cols], axis=-1, keepdims=True) for n in range(n_blk)]
            own = jnp.zeros((t, 1), jnp.int32) + i
            valid = [n < own for n in range(n_blk)]
            sel = _topk_select(gates, valid, MOBA_TOPK)
            sel_mat = jnp.zeros((t, n_blk), F32)
            for n in range(n_blk):
                sel_mat = sel_mat + jnp.where(blk == n, jnp.where(sel[n], 1.0, 0.0), 0.0)
            sel_ref[h] = sel_mat + jnp.where(blk == i, 1.0, 0.0)

    causal = _tile_mask(t, i, j, False)
    blk = lax.broadcasted_iota(jnp.int32, (t, n_blk), 1)
    for h in range(heads):
        cols = slice(h * HEAD_DIM, (h + 1) * HEAD_DIM)
        picked = jnp.sum(jnp.where(blk == j, sel_ref[h], 0.0), axis=-1, keepdims=True)
        sc = _dot(q_ref[:, cols].astype(BF16), kt_ref[cols, :].astype(BF16))
        sc = jnp.where(causal, sc, -jnp.inf) + jnp.where(picked > 0.0, 0.0, -jnp.inf)
        alpha, p = _online_softmax_step(sc, m_ref, l_ref, h)
        acc_ref[:, cols] = alpha * acc_ref[:, cols] + _dot_nt(p.astype(BF16), vt_ref[cols, :].astype(BF16))

    @pl.when(j == 0)
    def _():
        for h in range(heads):
            cols = slice(h * HEAD_DIM, (h + 1) * HEAD_DIM)
            o_ref[:, cols] = acc_ref[:, cols] / l_ref[h]


def _prompt_attention(kind, q, kv, layer, extra):
    b, s, w = q.shape
    t = ATTN_TILE
    n = s // t
    qi, kj = _causal_pairs(n)
    q_spec = pl.BlockSpec((None, t, w), lambda bb, st, qi, kj: (bb, qi[st], 0))
    o_spec = pl.BlockSpec((None, t, w), lambda bb, st, qi, kj: (bb, qi[st], 0))
    kt_spec = pl.BlockSpec((None, w, t), lambda bb, st, qi, kj: (bb, 0, kj[st]))
    vt_spec = pl.BlockSpec((None, w, t), lambda bb, st, qi, kj: (bb, 1, kj[st]))
    if kind == "sb":
        heads = w // HEAD_DIM
        body = functools.partial(_sb_prompt_kernel, heads=heads)
        in_specs, args = [q_spec, kt_spec, vt_spec], (q, kv, kv)
        scratch = [pltpu.VMEM((t, w), F32), pltpu.VMEM((heads, t, 1), F32)]
    elif kind == "diff":
        heads = w // (2 * HEAD_DIM)
        lp, g = extra
        body = functools.partial(_diff_prompt_kernel, heads=heads, lam_init=_lambda_init(layer))
        in_specs = [q_spec,
                    pl.BlockSpec((None, t) + kv.shape[2:], lambda bb, st, qi, kj: (bb, kj[st], 0, 0)),
                    pl.BlockSpec((None, 4, HEAD_DIM), lambda bb, st, qi, kj: (layer, 0, 0)),
                    pl.BlockSpec((None, 1, 2 * HEAD_DIM), lambda bb, st, qi, kj: (layer, 0, 0))]
        args = (q, kv, lp, g)
        scratch = [pltpu.VMEM((2, t, w), F32), pltpu.VMEM((2 * heads, t, 1), F32), pltpu.VMEM((2 * heads, t, 1), F32)]
    else:
        heads = w // HEAD_DIM
        kmean = extra
        body = functools.partial(_moba_prompt_kernel, heads=heads)
        in_specs = [q_spec, kt_spec, vt_spec, pl.BlockSpec((None, n, w), lambda bb, st, qi, kj: (bb, 0, 0))]
        args = (q, kv, kv, kmean)
        scratch = [pltpu.VMEM((t, w), F32), pltpu.VMEM((heads, t, 1), F32), pltpu.VMEM((heads, t, 1), F32),
                   pltpu.VMEM((heads, t, n), F32)]
    return pl.pallas_call(
        body,
        grid_spec=pltpu.PrefetchScalarGridSpec(
            num_scalar_prefetch=2, grid=(b, int(qi.shape[0])),
            in_specs=in_specs, out_specs=o_spec, scratch_shapes=scratch),
        out_shape=jax.ShapeDtypeStruct((b, s, w), F32),
        compiler_params=_params("parallel", "arbitrary"),
        name=kind + "_prompt_attention",
    )(qi, kj, *args)


def _block_diag_queries(q, groups, group_w):
    nq, w = q.shape
    rows = lax.broadcasted_iota(jnp.int32, (groups * nq, w), 0) // nq
    cols = lax.broadcasted_iota(jnp.int32, (groups * nq, w), 1) // group_w
    return jnp.where(rows == cols, jnp.concatenate([q] * groups, axis=0), 0.0)


def _head_diagonal(full, heads, nq, head_w):
    cols = lax.broadcasted_iota(jnp.int32, (nq, heads * head_w), 1) // head_w
    out = jnp.zeros((nq, heads * head_w), F32)
    for h in range(heads):
        out = jnp.where(cols == h, full[h * nq:(h + 1) * nq], out)
    return out


def _stage_new_tokens(pad_ref, kv_new):
    pad_ref[...] = jnp.zeros_like(pad_ref)
    pad_ref[0:kv_new.shape[0], :] = kv_new


def _feature_major_chunks(pad_ref, page_refs, ppc, w):
    yield True, pad_ref[:, :w].astype(BF16), pad_ref[:, w:].astype(BF16)
    for c in reversed(range(len(page_refs) // ppc)):
        refs = page_refs[c * ppc:(c + 1) * ppc]
        yield (False, jnp.concatenate([r[:w, :] for r in refs], axis=1).astype(BF16),
               jnp.concatenate([r[w:, :] for r in refs], axis=1).astype(BF16))


def _scores(qbd, k, token_major):
    return _dot_nt(qbd, k) if token_major else _dot(qbd, k)


def _weighted_values(p, v, token_major):
    return _dot(p, v) if token_major else _dot_nt(p, v)


def _new_token_masks(rows, nq, ck):
    r = lax.broadcasted_iota(jnp.int32, (rows, ck), 0) % nq
    c = lax.broadcasted_iota(jnp.int32, (rows, ck), 1)
    return c < r, c <= r


def _sb_sample_kernel(pt_ref, q_ref, kvn_ref, *rest, heads, n_pages, ppc):
    page_refs, o_ref, pad_ref = rest[:n_pages], rest[n_pages], rest[n_pages + 1]
    nq, w = q_ref.shape[0], q_ref.shape[-1]
    ck = pad_ref.shape[0]
    _stage_new_tokens(pad_ref, kvn_ref[:, 0, :])
    qbd = _block_diag_queries(q_ref[:, 0, :], heads, HEAD_DIM).astype(BF16)
    past_new, _ = _new_token_masks(heads * nq, nq, ck)
    upper = _strict_upper(ck)
    acc = jnp.zeros((heads * nq, w), F32)
    carry = jnp.zeros((heads * nq, 1), F32)
    for is_new, k, v in _feature_major_chunks(pad_ref, page_refs, ppc, w):
        z = _scores(qbd, k, is_new)
        sp = _softplus(z)
        log_rem = jnp.where(past_new, -sp, 0.0) if is_new else -sp
        after = _rev_excl_cumsum(log_rem, upper) + carry
        a = jnp.exp(z - sp + after)
        if is_new:
            a = jnp.where(past_new, a, 0.0)
        acc = acc + _weighted_values(a.astype(BF16), v, is_new)
        carry = carry + jnp.sum(log_rem, axis=-1, keepdims=True)
    o_ref[:, 0, :] = _head_diagonal(acc, heads, nq, HEAD_DIM)


def _moba_sample_kernel(pt_ref, q_ref, kvn_ref, *rest, heads, n_pages, ppc):
    page_refs, o_ref, pad_ref = rest[:n_pages], rest[n_pages], rest[n_pages + 1]
    nq, w = q_ref.shape[0], q_ref.shape[-1]
    ck = pad_ref.shape[0]
    _stage_new_tokens(pad_ref, kvn_ref[:, 0, :])
    qf = _block_diag_queries(q_ref[:, 0, :], heads, HEAD_DIM)
    qbd = qf.astype(BF16)
    rows = heads * nq
    _, causal_new = _new_token_masks(rows, nq, ck)
    n_blk = n_pages // ppc
    lane = lax.broadcasted_iota(jnp.int32, (w, LANES), 1)
    kmean = jnp.zeros((w, LANES), F32)
    for b in range(n_blk):
        k_sum = sum(jnp.sum(page_refs[b * ppc + p][:w, :], axis=-1, keepdims=True) for p in range(ppc))
        kmean = kmean + jnp.where(lane == b, k_sum * (1.0 / ck), 0.0)
    gate_mat = jnp.dot(qf, kmean, preferred_element_type=F32, precision=lax.Precision.HIGHEST)
    gates = [gate_mat[:, b:b + 1] for b in range(n_blk)]
    sel = _topk_select(gates, None, MOBA_TOPK)
    m = jnp.full((rows, 1), -jnp.inf, F32)
    l = jnp.zeros((rows, 1), F32)
    acc = jnp.zeros((rows, w), F32)
    blk = n_blk
    for is_new, k, v in _feature_major_chunks(pad_ref, page_refs, ppc, w):
        sc = _scores(qbd, k, is_new)
        if is_new:
            sc = jnp.where(causal_new, sc, -jnp.inf)
        else:
            blk -= 1
            sc = sc + jnp.where(sel[blk], 0.0, -jnp.inf)
        m_new = jnp.maximum(m, jnp.max(sc, axis=-1, keepdims=True))
        alpha = jnp.exp(m - m_new)
        p = jnp.exp(sc - m_new)
        l = alpha * l + jnp.sum(p, axis=-1, keepdims=True)
        acc = alpha * acc + _weighted_values(p.astype(BF16), v, is_new)
        m = m_new
    o_ref[:, 0, :] = _head_diagonal(acc / l, heads, nq, HEAD_DIM)


def _diff_sample_kernel(pt_ref, q_ref, kvn_ref, lp_ref, g_ref, *rest, heads, n_pages, ppc, lam_init):
    page_refs, o_ref, pad_ref = rest[:n_pages], rest[n_pages], rest[n_pages + 1]
    nq, w = q_ref.shape[0], q_ref.shape[-1]
    ck = pad_ref.shape[0]
    vw = 2 * HEAD_DIM

    def token_major(ref, first):
        return jnp.concatenate([ref[:, first + h, :] for h in range(heads)], axis=-1)

    _stage_new_tokens(pad_ref, jnp.concatenate([token_major(kvn_ref, 0), token_major(kvn_ref, heads)], axis=-1))
    groups = 2 * heads
    qg = _block_diag_queries(q_ref[:, 0, :], groups, HEAD_DIM)
    qbd = jnp.concatenate([qg[(2 * h + c) * nq:(2 * h + c + 1) * nq] for c in range(2) for h in range(heads)],
                          axis=0).astype(BF16)
    rows = groups * nq
    _, causal_new = _new_token_masks(rows, nq, ck)
    m = jnp.full((rows, 1), -jnp.inf, F32)
    l = jnp.zeros((rows, 1), F32)
    acc = jnp.zeros((rows, w), F32)
    chunks = [(True, pad_ref[:, :w], pad_ref[:, w:])]
    for c in reversed(range(n_pages // ppc)):
        refs = page_refs[c * ppc:(c + 1) * ppc]
        chunks.append((False, jnp.concatenate([token_major(r, 0) for r in refs], axis=0),
                       jnp.concatenate([token_major(r, heads) for r in refs], axis=0)))
    for is_new, k, v in chunks:
        sc = _dot_nt(qbd, k.astype(BF16))
        if is_new:
            sc = jnp.where(causal_new, sc, -jnp.inf)
        m_new = jnp.maximum(m, jnp.max(sc, axis=-1, keepdims=True))
        alpha = jnp.exp(m - m_new)
        p = jnp.exp(sc - m_new)
        l = alpha * l + jnp.sum(p, axis=-1, keepdims=True)
        acc = alpha * acc + _dot(p.astype(BF16), v.astype(BF16))
        m = m_new
    o = acc / l
    half = heads * nq
    lam = _diff_lambda(lp_ref[...], lam_init)
    od = _head_diagonal(o[:half] - lam * o[half:], heads, nq, vw)
    for h in range(heads):
        o_ref[:, 0, h * vw:(h + 1) * vw] = _rms(od[:, h * vw:(h + 1) * vw], g_ref[...]) * (1.0 - lam_init)


def _sample_attention(kind, q, kv_new, cache, page_table_flat, n_pages, layer, extra):
    nq, db, _, w = q.shape
    page = cache.shape[2] if kind == "diff" else cache.shape[3]
    ppc = DEC_CHUNK // page
    assert ppc * page == DEC_CHUNK and n_pages % ppc == 0 and nq <= DEC_CHUNK
    tok = lambda a: pl.BlockSpec((nq, None) + a.shape[2:], lambda b, pt: (0, b, 0, 0))
    page_specs = [pl.BlockSpec((None, None) + cache.shape[2:],
                               lambda b, pt, p=p: (layer, pt[b * n_pages + p]) + (0,) * (cache.ndim - 2))
                  for p in range(n_pages)]
    lead_specs, lead_args = [tok(q), tok(kv_new)], (q, kv_new)
    if kind == "sb":
        body = functools.partial(_sb_sample_kernel, heads=w // HEAD_DIM, n_pages=n_pages, ppc=ppc)
    elif kind == "moba":
        body = functools.partial(_moba_sample_kernel, heads=w // HEAD_DIM, n_pages=n_pages, ppc=ppc)
    else:
        lp, g = extra
        body = functools.partial(_diff_sample_kernel, heads=w // (2 * HEAD_DIM), n_pages=n_pages, ppc=ppc,
                                 lam_init=_lambda_init(layer))
        lead_specs += [pl.BlockSpec((None, 4, HEAD_DIM), lambda b, pt: (layer, 0, 0)),
                       pl.BlockSpec((None, 1, 2 * HEAD_DIM), lambda b, pt: (layer, 0, 0))]
        lead_args += (lp, g)
    return pl.pallas_call(
        body,
        grid_spec=pltpu.PrefetchScalarGridSpec(
            num_scalar_prefetch=1, grid=(db,),
            in_specs=lead_specs + page_specs, out_specs=tok(q),
            scratch_shapes=[pltpu.VMEM((DEC_CHUNK, 2 * w), F32)]),
        out_shape=jax.ShapeDtypeStruct(q.shape, F32),
        compiler_params=_params("parallel"),
        name=kind + "_sample_attention",
    )(page_table_flat, *lead_args, *([cache] * n_pages))


def _merge_kernel(x_ref, osb_ref, odf_ref, omb_ref, gates_ref, b_ref, wsb_ref, wdf_ref, wmb_ref, wout_ref, o_ref):
    d = x_ref.shape[1]
    merged = jnp.zeros(x_ref.shape, F32)
    for n, (o_r, w_r) in enumerate(((osb_ref, wsb_ref), (odf_ref, wdf_ref), (omb_ref, wmb_ref))):
        y = _dot(o_r[...].astype(BF16), w_r[...])
        gate = jax.nn.sigmoid(gates_ref[:, n * d:(n + 1) * d] + b_ref[n:n + 1, :])
        merged = merged + gate * y
    o_ref[...] = x_ref[...] + _dot(merged.astype(BF16), wout_ref[...])


def _merge(x, o_sb, o_df, o_mb, gates, b_gate, w_sb, w_df, w_mb, w_out, layer, tm):
    t, d = x.shape
    rows = lambda w: pl.BlockSpec((tm, w), lambda i: (i, 0))
    lay = lambda a: pl.BlockSpec((None,) + a.shape[1:], lambda i: (layer, 0, 0))
    return pl.pallas_call(
        _merge_kernel,
        grid=(t // tm,),
        in_specs=[rows(d), rows(o_sb.shape[1]), rows(o_df.shape[1]), rows(o_mb.shape[1]), rows(gates.shape[1]),
                  lay(b_gate), lay(w_sb), lay(w_df), lay(w_mb), lay(w_out)],
        out_specs=rows(d),
        out_shape=jax.ShapeDtypeStruct((t, d), F32),
        compiler_params=_params("parallel"),
        name="merge",
    )(x, o_sb, o_df, o_mb, gates, b_gate, w_sb, w_df, w_mb, w_out)


def _mlp_kernel(x_ref, g_ref, wup_ref, wdn_ref, gf_ref, o_ref, hn_ref, acc_ref, *, final_norm):
    f = pl.program_id(1)

    @pl.when(f == 0)
    def _():
        hn_ref[...] = _rms(x_ref[...], g_ref[...]).astype(BF16)
        acc_ref[...] = x_ref[...]

    up = _dot(hn_ref[...], wup_ref[...])
    act = jnp.square(jnp.maximum(up, 0.0))
    acc_ref[...] += _dot(act.astype(BF16), wdn_ref[...])

    @pl.when(f == pl.num_programs(1) - 1)
    def _():
        y = acc_ref[...]
        o_ref[...] = _rms(y, gf_ref[...]) if final_norm else y


def _mlp(x, g_mlp, w_up, w_down, g_final, layer, final_norm):
    t, d = x.shape
    ff = w_up.shape[-1]
    tm = min(MLP_ROW_TILE, t)
    tf = MLP_FF_TILE
    return pl.pallas_call(
        functools.partial(_mlp_kernel, final_norm=final_norm),
        grid=(t // tm, ff // tf),
        in_specs=[pl.BlockSpec((tm, d), lambda i, f: (i, 0)),
                  pl.BlockSpec((None, 1, d), lambda i, f: (layer, 0, 0)),
                  pl.BlockSpec((None, d, tf), lambda i, f: (layer, 0, f)),
                  pl.BlockSpec((None, tf, d), lambda i, f: (layer, f, 0)),
                  pl.BlockSpec((1, d), lambda i, f: (0, 0))],
        out_specs=pl.BlockSpec((tm, d), lambda i, f: (i, 0)),
        out_shape=jax.ShapeDtypeStruct((t, d), F32),
        scratch_shapes=[pltpu.VMEM((tm, d), BF16), pltpu.VMEM((tm, d), F32)],
        compiler_params=_params("parallel", "arbitrary"),
        name="mlp",
    )(x, g_mlp, w_up, w_down, g_final)


def _split_w_in(w_in, widths):
    sb_w, df_w, mb_w, gate_w = widths
    o_df = 3 * sb_w
    o_mb = o_df + 3 * df_w
    o_g = o_mb + 3 * mb_w
    w_tok = jnp.concatenate([w_in[..., :sb_w], w_in[..., o_df:o_mb + mb_w], w_in[..., o_g:o_g + gate_w]], axis=-1)
    w_feat = jnp.concatenate([w_in[..., sb_w:o_df], w_in[..., o_mb + mb_w:o_g]], axis=-1)
    return w_tok.astype(BF16), jnp.swapaxes(w_feat, 1, 2).astype(BF16)


def _trunk(x, tile_pos, tm, seq_tiles, caches, page_table, weights, widths):
    (g_mix, w_tok, w_feat, b_gate, diff_lambda, diff_subln_g, w_br_sb, w_br_diff, w_br_moba, w_out,
     g_mlp, w_up, w_down, g_final) = weights
    t, d = x.shape
    depth = w_tok.shape[0]
    sb_w, df_w, mb_w, _ = widths
    n_seq = t // (tm * seq_tiles)
    s = tm * seq_tiles
    assert t % tm == 0 and t % min(MLP_ROW_TILE, t) == 0 and w_up.shape[-1] % MLP_FF_TILE == 0
    tables = _rope_tables(tile_pos)
    if caches is not None:
        n_pages = page_table.shape[1]
        pt_flat = page_table.reshape(-1)
    new_sb, new_df, new_mb = [], [], []
    for layer in range(depth):
        q_sb, kv_sb, q_df, kv_df, q_mb, kv_mb, gates, kmean = _in_projection(
            x, g_mix, w_tok, w_feat, layer, tables, widths, tm, seq_tiles)
        if caches is None:
            r3 = lambda a: a.reshape(n_seq, s, a.shape[-1])
            o_sb = _prompt_attention("sb", r3(q_sb), kv_sb, layer, None)
            o_df = _prompt_attention("diff", r3(q_df), kv_df.reshape((n_seq, s) + kv_df.shape[1:]), layer,
                                     (diff_lambda, diff_subln_g))
            o_mb = _prompt_attention("moba", r3(q_mb), kv_mb, layer, kmean.reshape(n_seq, s // MOBA_BLOCK, mb_w))
        else:
            r4 = lambda a: a.reshape(n_seq, tm, 1, a.shape[-1])
            tok_major = lambda kv_t: jnp.swapaxes(kv_t, 1, 2).reshape(n_seq, tm, 1, kv_t.shape[1])
            o_sb = _sample_attention("sb", r4(q_sb), tok_major(kv_sb), caches[0], pt_flat, n_pages, layer, None)
            o_df = _sample_attention("diff", r4(q_df), kv_df.reshape((n_seq, tm) + kv_df.shape[1:]), caches[1],
                                     pt_flat, n_pages, layer, (diff_lambda, diff_subln_g))
            o_mb = _sample_attention("moba", r4(q_mb), tok_major(kv_mb), caches[2], pt_flat, n_pages, layer, None)
        x = _merge(x, o_sb.reshape(t, sb_w), o_df.reshape(t, df_w), o_mb.reshape(t, mb_w), gates,
                   b_gate, w_br_sb, w_br_diff, w_br_moba, w_out, layer, tm)
        x = _mlp(x, g_mlp, w_up, w_down, g_final, layer, layer == depth - 1)
        new_sb.append(kv_sb)
        new_df.append(kv_df)
        new_mb.append(kv_mb)
    return x, jnp.stack(new_sb), jnp.stack(new_df), jnp.stack(new_mb)


def kernel(x_prompt, x_sample, cache_sb_kv, cache_diff_kv, cache_moba_kv, page_table, g_mix, w_in, b_gate, diff_lambda, diff_subln_g, w_br_sb, w_br_diff, w_br_moba, w_out, g_mlp, w_up, w_down, g_final):
    depth, d = w_in.shape[0], w_in.shape[1]
    bp, sp = x_prompt.shape[:2]
    bs, ss = x_sample.shape[:2]
    page = cache_sb_kv.shape[2]
    past_len = page_table.shape[1] * page
    widths = (w_br_sb.shape[1], w_br_diff.shape[1], w_br_moba.shape[1], N_BRANCHES * d)
    assert sp % ATTN_TILE == 0 and ATTN_TILE == MOBA_BLOCK == PROMPT_ROW_TILE
    assert past_len % MOBA_BLOCK == 0 and ss <= MOBA_BLOCK and DEC_CHUNK == MOBA_BLOCK and bs % 8 == 0

    bf = lambda a: a.astype(BF16)
    row = lambda a: a.reshape(depth, 1, a.shape[-1])
    w_tok, w_feat = _split_w_in(w_in, widths)
    weights = (row(g_mix), w_tok, w_feat, b_gate, diff_lambda, row(diff_subln_g), bf(w_br_sb), bf(w_br_diff),
               bf(w_br_moba), bf(w_out), row(g_mlp), bf(w_up), bf(w_down), g_final.reshape(1, -1))

    feat_major = lambda c: jnp.transpose(c, (0, 1, 3, 4, 5, 2)).reshape(c.shape[:2] + (-1, c.shape[2]))
    caches = (feat_major(cache_sb_kv), cache_diff_kv.reshape(cache_diff_kv.shape[:3] + (-1, cache_diff_kv.shape[-1])),
              feat_major(cache_moba_kv))

    pos_p = jnp.arange(sp, dtype=jnp.int32)
    y_p, sb_p, df_p, mb_p = _trunk(x_prompt.reshape(bp * sp, d), pos_p, PROMPT_ROW_TILE, sp // PROMPT_ROW_TILE,
                                   None, None, weights, widths)
    pos_s = jnp.repeat(past_len + jnp.arange(ss, dtype=jnp.int32), bs)
    x_s = jnp.swapaxes(x_sample, 0, 1).reshape(ss * bs, d)
    y_s, sb_s, df_s, mb_s = _trunk(x_s, pos_s, bs, 1, caches, page_table, weights, widths)

    def kv_prompt(kv_t, ref_cache):
        a = kv_t.reshape((depth, bp) + ref_cache.shape[3:] + (sp,))
        return jnp.transpose(a, (0, 1, 5, 2, 3, 4))

    def kv_sample(kv_t, ref_cache):
        a = kv_t.reshape((depth, ss) + ref_cache.shape[3:] + (bs,))
        return jnp.transpose(a, (0, 5, 1, 2, 3, 4))

    df_shape = cache_diff_kv.shape[3:]
    return (y_p.reshape(bp, sp, d), jnp.swapaxes(y_s.reshape(ss, bs, d), 0, 1),
            kv_prompt(sb_p, cache_sb_kv), kv_sample(sb_s, cache_sb_kv),
            df_p.reshape((depth, bp, sp) + df_shape),
            jnp.swapaxes(df_s.reshape((depth, ss, bs) + df_shape), 1, 2),
            kv_prompt(mb_p, cache_moba_kv), kv_sample(mb_s, cache_moba_kv))
```

```python
import functools
import math

import jax
import jax.numpy as jnp
from jax import lax
from jax.experimental import pallas as pl
from jax.experimental.pallas import tpu as pltpu

F32 = jnp.float32
BF16 = jnp.bfloat16

HEAD_DIM = 64
ROPE_THETA = 10000.0
NORM_EPS = 1e-6
MOBA_BLOCK = 256
MOBA_TOPK = 3
N_BRANCHES = 3

LANES = 128
VMEM_LIMIT_BYTES = 56 * 1024 * 1024

PROMPT_ROW_TILE = 256
ATTN_TILE = 256
MLP_ROW_TILE = 512
MLP_FF_TILE = 1024
DEC_CHUNK = 256

Q_SCALE = math.log2(math.e) * HEAD_DIM ** -0.5
MASKED = -1e30


def _lambda_init(layer):
    return 0.8 - 0.6 * math.exp(-0.3 * layer)


def _params(*semantics):
    return pltpu.CompilerParams(dimension_semantics=semantics, vmem_limit_bytes=VMEM_LIMIT_BYTES)


def _dot(a, b):
    return jnp.dot(a, b, preferred_element_type=F32)


def _dot_nt(a, b, precision=None):
    return lax.dot_general(a, b, (((1,), (1,)), ((), ())), preferred_element_type=F32, precision=precision)


def _rms(x, g):
    return (x * lax.rsqrt(jnp.mean(x * x, axis=-1, keepdims=True) + NORM_EPS)) * g


def _strict_upper(n):
    r = lax.broadcasted_iota(jnp.int32, (n, n), 0)
    c = lax.broadcasted_iota(jnp.int32, (n, n), 1)
    return (r > c).astype(BF16)


def _sb_log_rem(z, past):
    nz = -z
    log_rem = jnp.minimum(nz, 0.0) - jnp.log2(1.0 + jnp.exp2(jnp.minimum(z, nz)))
    return log_rem if past is None else jnp.where(past, log_rem, 0.0)


def _sb_weights(z, log_rem, later, past):
    a = jnp.exp2(z + log_rem + later)
    if past is not None:
        a = jnp.where(past, a, 0.0)
    return a.astype(BF16), later[:, 0:1] + log_rem[:, 0:1]


def _row_max(sc):
    return jnp.max(sc, axis=-1, keepdims=True)


def _rope_tables(pos):
    half = HEAD_DIM // 2
    inv_freq = ROPE_THETA ** (-jnp.arange(half, dtype=F32) / half)
    ang = pos.astype(F32)[:, None] * inv_freq[None, :]
    cos, sin = jnp.cos(ang), jnp.sin(ang)
    reps = LANES // HEAD_DIM
    return (jnp.tile(jnp.concatenate([cos, cos], -1), (1, reps)),
            jnp.tile(jnp.concatenate([-sin, sin], -1), (1, reps)), cos.T, sin.T)


def _rope(v, cos, sin_signed):
    half = HEAD_DIM // 2
    first = (lax.broadcasted_iota(jnp.int32, (v.shape[0], LANES), 1) % HEAD_DIM) < half
    outs = []
    for c in range(v.shape[1] // LANES):
        s = v[:, c * LANES:(c + 1) * LANES]
        swapped = jnp.where(first, pltpu.roll(s, LANES - half, 1), pltpu.roll(s, half, 1))
        outs.append(s * cos + swapped * sin_signed)
    return outs[0] if len(outs) == 1 else jnp.concatenate(outs, axis=-1)


def _rope_feature_major(vt, cos_t, sin_t):
    half = HEAD_DIM // 2
    outs = []
    for h in range(vt.shape[0] // HEAD_DIM):
        x1 = vt[h * HEAD_DIM:h * HEAD_DIM + half]
        x2 = vt[h * HEAD_DIM + half:(h + 1) * HEAD_DIM]
        outs += [x1 * cos_t - x2 * sin_t, x2 * cos_t + x1 * sin_t]
    return jnp.concatenate(outs, axis=0)


def _proj_kernel(x_ref, g_ref, w_ref, wt_ref, cos_ref, sin_ref, cost_ref, sint_ref,
                 qsb_ref, qdf_ref, qmb_ref, kdf_bf_ref, vdf_bf_ref, kvsb_bf_ref, kvmb_bf_ref,
                 kvsb_ref, kvdf_ref, kvmb_ref, gates_ref, kmean_ref,
                 *, sb_w, df_w, mb_w, gate_w):
    xb = _rms(x_ref[...], g_ref[...]).astype(BF16)
    cos, sin = cos_ref[...], sin_ref[...]

    def mm(lo, width):
        return _dot(xb, w_ref[:, lo:lo + width])

    o = 0
    qsb_ref[...] = (mm(o, sb_w) * Q_SCALE).astype(BF16)
    o += sb_w
    qdf_ref[...] = (_rope(mm(o, df_w), cos, sin) * Q_SCALE).astype(BF16)
    o += df_w
    k_df = _rope(mm(o, df_w), cos, sin)
    o += df_w
    v_df = mm(o, df_w)
    o += df_w
    kdf_bf_ref[...] = k_df.astype(BF16)
    vdf_bf_ref[...] = v_df.astype(BF16)
    groups = df_w // LANES
    for j in range(groups):
        kvdf_ref[:, j, :] = k_df[:, j * LANES:(j + 1) * LANES]
        kvdf_ref[:, groups + j, :] = v_df[:, j * LANES:(j + 1) * LANES]
    qmb_ref[...] = (_rope(mm(o, mb_w), cos, sin) * Q_SCALE).astype(BF16)
    o += mb_w
    chunk = 1024
    for c in range(0, gate_w, chunk):
        gates_ref[:, c:c + chunk] = mm(o + c, chunk)

    kv_t = _dot_nt(wt_ref[...], xb)
    kv_sb = kv_t[:2 * sb_w]
    kvsb_ref[...] = kv_sb
    kvsb_bf_ref[...] = kv_sb.astype(BF16)
    k_mb = _rope_feature_major(kv_t[2 * sb_w:2 * sb_w + mb_w], cost_ref[...], sint_ref[...])
    v_mb = kv_t[2 * sb_w + mb_w:]
    kvmb_ref[:mb_w] = k_mb
    kvmb_ref[mb_w:] = v_mb
    kvmb_bf_ref[:mb_w] = k_mb.astype(BF16)
    kvmb_bf_ref[mb_w:] = v_mb.astype(BF16)
    kmean_ref[...] = jnp.mean(k_mb, axis=-1, keepdims=True)


def _in_projection(x, g, w_tok, w_feat, layer, tables, widths, tm, seq_tiles):
    t, d = x.shape
    sb_w, df_w, mb_w, gate_w = widths
    cos, sin, cos_t, sin_t = tables
    tab_tiles = cos.shape[0] // tm
    n_tiles = t // tm
    n_seq = n_tiles // seq_tiles
    s = seq_tiles * tm
    half = HEAD_DIM // 2
    kv_groups = 2 * df_w // LANES

    def rows(w):
        return pl.BlockSpec((tm, w), lambda i: (i, 0))

    def feat(w):
        return pl.BlockSpec((None, w, tm), lambda i: (i // seq_tiles, 0, i % seq_tiles))

    out_shapes = [
        jax.ShapeDtypeStruct((t, sb_w), BF16), jax.ShapeDtypeStruct((t, df_w), BF16),
        jax.ShapeDtypeStruct((t, mb_w), BF16),
        jax.ShapeDtypeStruct((t, df_w), BF16), jax.ShapeDtypeStruct((t, df_w), BF16),
        jax.ShapeDtypeStruct((n_seq, 2 * sb_w, s), BF16), jax.ShapeDtypeStruct((n_seq, 2 * mb_w, s), BF16),
        jax.ShapeDtypeStruct((n_seq, 2 * sb_w, s), F32),
        jax.ShapeDtypeStruct((t, kv_groups, LANES), F32),
        jax.ShapeDtypeStruct((n_seq, 2 * mb_w, s), F32),
        jax.ShapeDtypeStruct((t, gate_w), F32),
        jax.ShapeDtypeStruct((n_tiles, mb_w, 1), F32),
    ]
    out_specs = [rows(sb_w), rows(df_w), rows(mb_w), rows(df_w), rows(df_w), feat(2 * sb_w), feat(2 * mb_w),
                 feat(2 * sb_w), pl.BlockSpec((tm, kv_groups, LANES), lambda i: (i, 0, 0)), feat(2 * mb_w),
                 rows(gate_w), pl.BlockSpec((None, mb_w, 1), lambda i: (i, 0, 0))]
    return pl.pallas_call(
        functools.partial(_proj_kernel, sb_w=sb_w, df_w=df_w, mb_w=mb_w, gate_w=gate_w),
        grid=(n_tiles,),
        in_specs=[rows(d),
                  pl.BlockSpec((None, 1, d), lambda i: (layer, 0, 0)),
                  pl.BlockSpec((None, d, w_tok.shape[-1]), lambda i: (layer, 0, 0)),
                  pl.BlockSpec((None, w_feat.shape[1], d), lambda i: (layer, 0, 0)),
                  pl.BlockSpec((tm, LANES), lambda i: (i % tab_tiles, 0)),
                  pl.BlockSpec((tm, LANES), lambda i: (i % tab_tiles, 0)),
                  pl.BlockSpec((half, tm), lambda i: (0, i % tab_tiles)),
                  pl.BlockSpec((half, tm), lambda i: (0, i % tab_tiles))],
        out_specs=out_specs,
        out_shape=out_shapes,
        compiler_params=_params("parallel"),
        name="in_projection",
    )(x, g, w_tok, w_feat, cos, sin, cos_t, sin_t)


def _causal_pairs(n):
    qi = [i for i in range(n) for _ in range(i + 1)]
    kj = [i - j for i in range(n) for j in range(i + 1)]
    return jnp.asarray(qi, jnp.int32), jnp.asarray(kj, jnp.int32)


def _diag_mask(t, strict):
    r = lax.broadcasted_iota(jnp.int32, (t, t), 0)
    c = lax.broadcasted_iota(jnp.int32, (t, t), 1)
    return (c < r) if strict else (c <= r)


def _sb_prompt_kernel(qi_ref, kj_ref, q_ref, kt_ref, vt_ref, o_ref, acc_ref, carry_ref, *, heads):
    s = pl.program_id(1)
    i, j = qi_ref[s], kj_ref[s]
    t = q_ref.shape[0]

    def sweep(past):
        upper = _strict_upper(t)
        hs = range(heads)
        cols = [slice(h * HEAD_DIM, (h + 1) * HEAD_DIM) for h in hs]
        zs = [_dot(q_ref[:, cols[h]], kt_ref[cols[h], :]) for h in hs]
        lrs = [_sb_log_rem(zs[h], past) for h in hs]
        laters = [_dot(lrs[h].astype(BF16), upper) for h in hs]
        weights = [_sb_weights(zs[h], lrs[h], laters[h], past) for h in hs]
        outs = [_dot_nt(weights[h][0], vt_ref[cols[h], :]) for h in hs]
        for h in hs:
            carry = carry_ref[h]
            acc_ref[:, cols[h]] += jnp.exp2(carry) * outs[h]
            carry_ref[h] = carry + weights[h][1]

    @pl.when(j == i)
    def _():
        acc_ref[...] = jnp.zeros_like(acc_ref)
        carry_ref[...] = jnp.zeros_like(carry_ref)
        sweep(_diag_mask(t, True))

    @pl.when(j != i)
    def _():
        sweep(None)

    @pl.when(j == 0)
    def _():
        o_ref[...] = acc_ref[...]


def _diff_lambda(lp, lam_init):
    a = jnp.sum(lp[0:1] * lp[1:2], axis=-1, keepdims=True)
    b = jnp.sum(lp[2:3] * lp[3:4], axis=-1, keepdims=True)
    return jnp.exp(a) - jnp.exp(b) + lam_init


def _diff_prompt_kernel(qi_ref, kj_ref, q_ref, k_ref, v_ref, lp_ref, g_ref, o_ref, acc_ref, m_ref,
                        *, heads, lam_init):
    s = pl.program_id(1)
    i, j = qi_ref[s], kj_ref[s]
    t = q_ref.shape[0]
    vw = 2 * HEAD_DIM

    def sweep(causal):
        idxs = [(h, c) for h in range(heads) for c in range(2)]
        scs = []
        for h, c in idxs:
            cols = slice(h * vw + c * HEAD_DIM, h * vw + (c + 1) * HEAD_DIM)
            sc = _dot_nt(q_ref[:, cols], k_ref[:, cols])
            if causal is not None:
                sc = jnp.where(causal, sc, -jnp.inf)
            scs.append(sc)
        alphas, ps = [], []
        for n in range(len(idxs)):
            m_old = m_ref[n]
            m_new = jnp.maximum(m_old, _row_max(scs[n]))
            alphas.append(jnp.exp2(m_old - m_new))
            ps.append(jnp.exp2(scs[n] - jnp.concatenate([m_new] * (t // LANES), axis=1)).astype(BF16))
            m_ref[n] = m_new
        ones = jnp.ones((t, vw), BF16)
        pvs = [_dot(jnp.concatenate(ps[2 * h:2 * h + 2], axis=0),
                    jnp.concatenate([v_ref[:, h * vw:(h + 1) * vw], ones], axis=1)) for h in range(heads)]
        for h in range(heads):
            alpha = jnp.concatenate(alphas[2 * h:2 * h + 2], axis=0)
            acc_ref[h] = jnp.concatenate([alpha] * (2 * vw // LANES), axis=1) * acc_ref[h] + pvs[h]

    @pl.when(j == i)
    def _():
        acc_ref[...] = jnp.zeros_like(acc_ref)
        m_ref[...] = jnp.full_like(m_ref, -jnp.inf)
        sweep(_diag_mask(t, False))

    @pl.when(j != i)
    def _():
        sweep(None)

    @pl.when(j == 0)
    def _():
        lam = _diff_lambda(lp_ref[...], lam_init)
        for h in range(heads):
            o1 = acc_ref[h, 0:t, 0:vw] / acc_ref[h, 0:t, vw:vw + 1]
            o2 = acc_ref[h, t:2 * t, 0:vw] / acc_ref[h, t:2 * t, vw:vw + 1]
            o_ref[:, h * vw:(h + 1) * vw] = _rms(o1 - lam * o2, g_ref[...]) * (1.0 - lam_init)


def _moba_prompt_kernel(qi_ref, kj_ref, q_ref, kt_ref, vt_ref, kmean_ref, o_ref, acc_ref, m_ref, qaug_ref,
                        *, heads, n_blk):
    s = pl.program_id(1)
    i, j = qi_ref[s], kj_ref[s]
    t = q_ref.shape[0]
    gate_rows = ((n_blk + 7) // 8) * 8

    def select_blocks():
        lane = lax.broadcasted_iota(jnp.int32, (t, LANES), 1)
        blk = lax.broadcasted_iota(jnp.int32, (gate_rows, t), 0)
        for h in range(heads):
            qh = q_ref[:, h * HEAD_DIM:(h + 1) * HEAD_DIM].astype(F32)
            gate = _dot_nt(kmean_ref[h], qh, precision=lax.Precision.HIGHEST)[HEAD_DIM:HEAD_DIM + gate_rows]
            rank = jnp.zeros((gate_rows, t), F32)
            for m in range(n_blk):
                gm = gate[m:m + 1, :]
                beats = jnp.logical_or(gm > gate, jnp.logical_and(gm == gate, blk > m))
                rank = rank + jnp.where(beats, 1.0, 0.0) * jnp.where(m < i, 1.0, 0.0)
            attended = jnp.logical_or(jnp.logical_and(rank < MOBA_TOPK, blk < i), blk == i)
            att = jnp.concatenate([jnp.zeros((HEAD_DIM, t), F32), jnp.where(attended, 1.0, 0.0),
                                   jnp.zeros((LANES - HEAD_DIM - gate_rows, t), F32)], axis=0).T
            pair = q_ref[:, (h // 2) * LANES:(h // 2 + 1) * LANES].astype(F32)
            if h % 2:
                pair = pltpu.roll(pair, HEAD_DIM, 1)
            in_gate_lanes = jnp.logical_and(lane >= HEAD_DIM, lane < HEAD_DIM + n_blk)
            pen = jnp.where(jnp.logical_and(in_gate_lanes, att < 0.5), MASKED, 0.0)
            qaug_ref[h] = jnp.where(lane < HEAD_DIM, pair, pen).astype(BF16)

    def sweep(causal):
        rows = lax.broadcasted_iota(jnp.int32, (LANES - HEAD_DIM, t), 0)
        one_hot = jnp.where(rows == j, 1.0, 0.0).astype(BF16)
        ones = jnp.ones((LANES - HEAD_DIM, t), BF16)
        hs = range(heads)
        cols = [slice(h * HEAD_DIM, (h + 1) * HEAD_DIM) for h in hs]
        scs = []
        for h in hs:
            sc = _dot(qaug_ref[h], jnp.concatenate([kt_ref[cols[h], :], one_hot], axis=0))
            if causal is not None:
                sc = jnp.where(causal, sc, -jnp.inf)
            scs.append(sc)
        alphas, ps = [], []
        for h in hs:
            m_old = m_ref[h]
            m_new = jnp.maximum(m_old, _row_max(scs[h]))
            alphas.append(jnp.exp2(m_old - m_new))
            ps.append(jnp.exp2(scs[h] - jnp.concatenate([m_new] * (t // LANES), axis=1)).astype(BF16))
            m_ref[h] = m_new
        pvs = [_dot_nt(ps[h], jnp.concatenate([vt_ref[cols[h], :], ones], axis=0)) for h in hs]
        for h in hs:
            acc_ref[h] = alphas[h] * acc_ref[h] + pvs[h]

    @pl.when(j == i)
    def _():
        acc_ref[...] = jnp.zeros_like(acc_ref)
        m_ref[...] = jnp.full_like(m_ref, -jnp.inf)
        select_blocks()
        sweep(_diag_mask(t, False))

    @pl.when(j != i)
    def _():
        sweep(None)

    @pl.when(j == 0)
    def _():
        for h in range(heads):
            o_ref[:, h * HEAD_DIM:(h + 1) * HEAD_DIM] = acc_ref[h, :, 0:HEAD_DIM] / acc_ref[h, :, HEAD_DIM:HEAD_DIM + 1]


def _prompt_attention(kind, q, kv, layer, extra):
    b, s, w = q.shape
    t = ATTN_TILE
    n = s // t
    qi, kj = _causal_pairs(n)
    q_spec = pl.BlockSpec((None, t, w), lambda bb, st, qi, kj: (bb, qi[st], 0))
    o_spec = pl.BlockSpec((None, t, w), lambda bb, st, qi, kj: (bb, qi[st], 0))
    kt_spec = pl.BlockSpec((None, w, t), lambda bb, st, qi, kj: (bb, 0, kj[st]))
    vt_spec = pl.BlockSpec((None, w, t), lambda bb, st, qi, kj: (bb, 1, kj[st]))
    if kind == "sb":
        heads = w // HEAD_DIM
        body = functools.partial(_sb_prompt_kernel, heads=heads)
        in_specs, args = [q_spec, kt_spec, vt_spec], (q, kv, kv)
        scratch = [pltpu.VMEM((t, w), F32), pltpu.VMEM((heads, t, 1), F32)]
    elif kind == "diff":
        heads = w // (2 * HEAD_DIM)
        lp, g = extra
        body = functools.partial(_diff_prompt_kernel, heads=heads, lam_init=_lambda_init(layer))
        tok_spec = pl.BlockSpec((None, t, w), lambda bb, st, qi, kj: (bb, kj[st], 0))
        in_specs = [q_spec, tok_spec, tok_spec,
                    pl.BlockSpec((None, 4, HEAD_DIM), lambda bb, st, qi, kj: (layer, 0, 0)),
                    pl.BlockSpec((None, 1, 2 * HEAD_DIM), lambda bb, st, qi, kj: (layer, 0, 0))]
        args = (q, kv[0], kv[1], lp, g)
        scratch = [pltpu.VMEM((heads, 2 * t, 4 * HEAD_DIM), F32), pltpu.VMEM((2 * heads, t, LANES), F32)]
    else:
        heads = w // HEAD_DIM
        assert HEAD_DIM + n <= LANES
        kmean = extra.reshape(b, n, heads, HEAD_DIM).transpose(0, 2, 1, 3)
        kmean = jnp.pad(kmean, ((0, 0), (0, 0), (HEAD_DIM, LANES - HEAD_DIM - n), (0, 0)))
        body = functools.partial(_moba_prompt_kernel, heads=heads, n_blk=n)
        in_specs = [q_spec, kt_spec, vt_spec,
                    pl.BlockSpec((None, heads, LANES, HEAD_DIM), lambda bb, st, qi, kj: (bb, 0, 0, 0))]
        args = (q, kv, kv, kmean)
        scratch = [pltpu.VMEM((heads, t, LANES), F32), pltpu.VMEM((heads, t, LANES), F32),
                   pltpu.VMEM((heads, t, LANES), BF16)]
    return pl.pallas_call(
        body,
        grid_spec=pltpu.PrefetchScalarGridSpec(
            num_scalar_prefetch=2, grid=(b, int(qi.shape[0])),
            in_specs=in_specs, out_specs=o_spec, scratch_shapes=scratch),
        out_shape=jax.ShapeDtypeStruct((b, s, w), F32),
        compiler_params=_params("parallel", "arbitrary"),
        name=kind + "_prompt_attention",
    )(qi, kj, *args)


def _block_diag_queries(q, groups, group_w):
    nq, w = q.shape
    rows = lax.broadcasted_iota(jnp.int32, (groups * nq, w), 0) // nq
    cols = lax.broadcasted_iota(jnp.int32, (groups * nq, w), 1) // group_w
    return jnp.where(rows == cols, jnp.concatenate([q] * groups, axis=0), 0.0)


def _head_diagonal(full, heads, nq, head_w):
    cols = lax.broadcasted_iota(jnp.int32, (nq, heads * head_w), 1) // head_w
    out = jnp.zeros((nq, heads * head_w), F32)
    for h in range(heads):
        out = jnp.where(cols == h, full[h * nq:(h + 1) * nq], out)
    return out


def _stage_new_tokens(pad_ref, kv_new):
    pad_ref[...] = jnp.zeros_like(pad_ref)
    pad_ref[0:kv_new.shape[0], :] = kv_new.astype(pad_ref.dtype)


def _feature_major_chunks(pad_ref, page_refs, ppc, w):
    yield True, pad_ref[:, :w].astype(BF16), pad_ref[:, w:].astype(BF16)
    for c in reversed(range(len(page_refs) // ppc)):
        refs = page_refs[c * ppc:(c + 1) * ppc]
        yield (False, jnp.concatenate([r[:w, :] for r in refs], axis=1).astype(BF16),
               jnp.concatenate([r[w:, :] for r in refs], axis=1).astype(BF16))


def _new_token_masks(rows, nq, ck):
    r = lax.broadcasted_iota(jnp.int32, (rows, ck), 0) % nq
    c = lax.broadcasted_iota(jnp.int32, (rows, ck), 1)
    return c < r, c <= r


def _sb_sample_kernel(pt_ref, q_ref, kvn_ref, *rest, heads, n_pages, ppc):
    page_refs, o_ref, pad_ref = rest[:n_pages], rest[n_pages], rest[n_pages + 1]
    nq, w = q_ref.shape[0], q_ref.shape[-1]
    ck = pad_ref.shape[0]
    _stage_new_tokens(pad_ref, kvn_ref[:, 0, :])
    qbd = _block_diag_queries(q_ref[:, 0, :], heads, HEAD_DIM).astype(BF16)
    past_new, _ = _new_token_masks(heads * nq, nq, ck)
    upper = _strict_upper(ck)
    chunks = list(_feature_major_chunks(pad_ref, page_refs, ppc, w))
    masks = [past_new if is_new else None for is_new, _, _ in chunks]
    n = range(len(chunks))
    zs = [_dot_nt(qbd, k) if is_new else _dot(qbd, k) for is_new, k, _ in chunks]
    lrs = [_sb_log_rem(zs[c], masks[c]) for c in n]
    laters = [_dot(lrs[c].astype(BF16), upper) for c in n]
    weights = [_sb_weights(zs[c], lrs[c], laters[c], masks[c]) for c in n]
    outs = [_dot(weights[c][0], chunks[c][2]) if chunks[c][0] else _dot_nt(weights[c][0], chunks[c][2]) for c in n]
    acc = jnp.zeros((heads * nq, w), F32)
    carry = jnp.zeros((heads * nq, 1), F32)
    for c in n:
        acc = acc + jnp.exp2(carry) * outs[c]
        carry = carry + weights[c][1]
    o_ref[:, 0, :] = _head_diagonal(acc, heads, nq, HEAD_DIM)


def _softmax_over_chunks(scs, values, token_major):
    m = functools.reduce(jnp.maximum, [_row_max(sc) for sc in scs])
    ps = [jnp.exp2(sc - m) for sc in scs]
    l = sum(jnp.sum(p, axis=-1, keepdims=True) for p in ps)
    outs = [_dot(p.astype(BF16), v) if tm else _dot_nt(p.astype(BF16), v) for p, v, tm in zip(ps, values, token_major)]
    return sum(outs) / l


def _moba_sample_kernel(pt_ref, q_ref, kvn_ref, *rest, heads, n_pages, ppc):
    page_refs, o_ref, pad_ref = rest[:n_pages], rest[n_pages], rest[n_pages + 1]
    nq, w = q_ref.shape[0], q_ref.shape[-1]
    ck = pad_ref.shape[0]
    _stage_new_tokens(pad_ref, kvn_ref[:, 0, :])
    qf = _block_diag_queries(q_ref[:, 0, :].astype(F32), heads, HEAD_DIM)
    qbd = qf.astype(BF16)
    rows = heads * nq
    _, causal_new = _new_token_masks(rows, nq, ck)
    n_blk = n_pages // ppc
    lane = lax.broadcasted_iota(jnp.int32, (w, LANES), 1)
    kmean = jnp.zeros((w, LANES), F32)
    for b in range(n_blk):
        k_sum = sum(jnp.sum(page_refs[b * ppc + p][:w, :], axis=-1, keepdims=True) for p in range(ppc))
        kmean = kmean + jnp.where(lane == b, k_sum * (1.0 / ck), 0.0)
    gate = jnp.dot(qf, kmean, preferred_element_type=F32, precision=lax.Precision.HIGHEST)
    blk_lane = lax.broadcasted_iota(jnp.int32, (rows, LANES), 1)
    rank = jnp.zeros((rows, LANES), F32)
    for m in range(n_blk):
        gm = gate[:, m:m + 1]
        beats = jnp.logical_or(gm > gate, jnp.logical_and(gm == gate, blk_lane > m))
        rank = rank + jnp.where(beats, 1.0, 0.0)
    pen = jnp.where(rank < MOBA_TOPK, 0.0, MASKED)
    chunks = list(_feature_major_chunks(pad_ref, page_refs, ppc, w))
    scs = []
    for c, (is_new, k, _) in enumerate(chunks):
        if is_new:
            scs.append(jnp.where(causal_new, _dot_nt(qbd, k), -jnp.inf))
        else:
            blk = n_blk - c
            scs.append(_dot(qbd, k) + pen[:, blk:blk + 1])
    o = _softmax_over_chunks(scs, [v for _, _, v in chunks], [is_new for is_new, _, _ in chunks])
    o_ref[:, 0, :] = _head_diagonal(o, heads, nq, HEAD_DIM)


def _diff_sample_kernel(pt_ref, q_ref, kn_ref, vn_ref, lp_ref, g_ref, *rest, heads, n_pages, ppc, lam_init):
    page_refs, o_ref, pad_ref = rest[:n_pages], rest[n_pages], rest[n_pages + 1]
    nq, w = q_ref.shape[0], q_ref.shape[-1]
    ck = pad_ref.shape[0]
    vw = 2 * HEAD_DIM
    groups = 2 * heads
    page = page_refs[0].shape[0] // groups

    def token_major(ref, first):
        return jnp.concatenate([ref[pl.ds(first + h, page, stride=groups), :] for h in range(heads)], axis=-1)

    _stage_new_tokens(pad_ref, jnp.concatenate([kn_ref[:, 0, :], vn_ref[:, 0, :]], axis=-1))
    qg = _block_diag_queries(q_ref[:, 0, :].astype(F32), groups, HEAD_DIM)
    qbd = jnp.concatenate([qg[(2 * h + c) * nq:(2 * h + c + 1) * nq] for c in range(2) for h in range(heads)],
                          axis=0).astype(BF16)
    rows = groups * nq
    _, causal_new = _new_token_masks(rows, nq, ck)
    scs = [jnp.where(causal_new, _dot_nt(qbd, pad_ref[:, :w].astype(BF16)), -jnp.inf)]
    values = [pad_ref[:, w:].astype(BF16)]
    for c in reversed(range(n_pages // ppc)):
        refs = page_refs[c * ppc:(c + 1) * ppc]
        k = jnp.concatenate([token_major(r, 0) for r in refs], axis=0).astype(BF16)
        scs.append(_dot_nt(qbd, k))
        values.append(jnp.concatenate([token_major(r, heads) for r in refs], axis=0).astype(BF16))
    o = _softmax_over_chunks(scs, values, [True] * len(scs))
    half = heads * nq
    lam = _diff_lambda(lp_ref[...], lam_init)
    od = _head_diagonal(o[:half] - lam * o[half:], heads, nq, vw)
    for h in range(heads):
        o_ref[:, 0, h * vw:(h + 1) * vw] = _rms(od[:, h * vw:(h + 1) * vw], g_ref[...]) * (1.0 - lam_init)


def _sample_attention(kind, q, kv_new, cache, page_table_flat, n_pages, layer, extra):
    nq, db, _, w = q.shape
    page = cache.shape[3] if kind != "diff" else cache.shape[2] * cache.shape[3] // (2 * w)
    ppc = DEC_CHUNK // page
    assert ppc * page == DEC_CHUNK and n_pages % ppc == 0 and nq <= DEC_CHUNK
    tok = lambda a: pl.BlockSpec((nq, None) + a.shape[2:], lambda b, pt: (0, b, 0, 0))
    page_specs = [pl.BlockSpec((None, None) + cache.shape[2:], lambda b, pt, p=p: (layer, pt[b * n_pages + p], 0, 0))
                  for p in range(n_pages)]
    lead_specs, lead_args = [tok(q)] + [tok(a) for a in kv_new], (q,) + tuple(kv_new)
    if kind == "sb":
        body = functools.partial(_sb_sample_kernel, heads=w // HEAD_DIM, n_pages=n_pages, ppc=ppc)
    elif kind == "moba":
        body = functools.partial(_moba_sample_kernel, heads=w // HEAD_DIM, n_pages=n_pages, ppc=ppc)
    else:
        lp, g = extra
        body = functools.partial(_diff_sample_kernel, heads=w // (2 * HEAD_DIM), n_pages=n_pages, ppc=ppc,
                                 lam_init=_lambda_init(layer))
        lead_specs += [pl.BlockSpec((None, 4, HEAD_DIM), lambda b, pt: (layer, 0, 0)),
                       pl.BlockSpec((None, 1, 2 * HEAD_DIM), lambda b, pt: (layer, 0, 0))]
        lead_args += (lp, g)
    return pl.pallas_call(
        body,
        grid_spec=pltpu.PrefetchScalarGridSpec(
            num_scalar_prefetch=1, grid=(db,),
            in_specs=lead_specs + page_specs, out_specs=tok(jax.ShapeDtypeStruct(q.shape, F32)),
            scratch_shapes=[pltpu.VMEM((DEC_CHUNK, 2 * w), F32)]),
        out_shape=jax.ShapeDtypeStruct(q.shape, F32),
        compiler_params=_params("parallel"),
        name=kind + "_sample_attention",
    )(page_table_flat, *lead_args, *([cache] * n_pages))


def _merge_kernel(x_ref, osb_ref, odf_ref, omb_ref, gates_ref, b_ref, wsb_ref, wdf_ref, wmb_ref, wout_ref, o_ref):
    d = x_ref.shape[1]
    merged = jnp.zeros(x_ref.shape, F32)
    for n, (o_r, w_r) in enumerate(((osb_ref, wsb_ref), (odf_ref, wdf_ref), (omb_ref, wmb_ref))):
        y = _dot(o_r[...].astype(BF16), w_r[...])
        gate = jax.nn.sigmoid(gates_ref[:, n * d:(n + 1) * d] + b_ref[n:n + 1, :])
        merged = merged + gate * y
    o_ref[...] = x_ref[...] + _dot(merged.astype(BF16), wout_ref[...])


def _merge(x, o_sb, o_df, o_mb, gates, b_gate, w_sb, w_df, w_mb, w_out, layer, tm):
    t, d = x.shape
    rows = lambda w: pl.BlockSpec((tm, w), lambda i: (i, 0))
    lay = lambda a: pl.BlockSpec((None,) + a.shape[1:], lambda i: (layer, 0, 0))
    return pl.pallas_call(
        _merge_kernel,
        grid=(t // tm,),
        in_specs=[rows(d), rows(o_sb.shape[1]), rows(o_df.shape[1]), rows(o_mb.shape[1]), rows(gates.shape[1]),
                  lay(b_gate), lay(w_sb), lay(w_df), lay(w_mb), lay(w_out)],
        out_specs=rows(d),
        out_shape=jax.ShapeDtypeStruct((t, d), F32),
        compiler_params=_params("parallel"),
        name="merge",
    )(x, o_sb, o_df, o_mb, gates, b_gate, w_sb, w_df, w_mb, w_out)


def _mlp_kernel(x_ref, g_ref, wup_ref, wdn_ref, gf_ref, o_ref, hn_ref, acc_ref, *, final_norm):
    f = pl.program_id(1)

    @pl.when(f == 0)
    def _():
        hn_ref[...] = _rms(x_ref[...], g_ref[...]).astype(BF16)
        acc_ref[...] = x_ref[...]

    up = _dot(hn_ref[...], wup_ref[...])
    act = jnp.square(jnp.maximum(up, 0.0))
    acc_ref[...] += _dot(act.astype(BF16), wdn_ref[...])

    @pl.when(f == pl.num_programs(1) - 1)
    def _():
        y = acc_ref[...]
        o_ref[...] = _rms(y, gf_ref[...]) if final_norm else y


def _mlp(x, g_mlp, w_up, w_down, g_final, layer, final_norm):
    t, d = x.shape
    ff = w_up.shape[-1]
    tm = min(MLP_ROW_TILE, t)
    tf = MLP_FF_TILE
    return pl.pallas_call(
        functools.partial(_mlp_kernel, final_norm=final_norm),
        grid=(t // tm, ff // tf),
        in_specs=[pl.BlockSpec((tm, d), lambda i, f: (i, 0)),
                  pl.BlockSpec((None, 1, d), lambda i, f: (layer, 0, 0)),
                  pl.BlockSpec((None, d, tf), lambda i, f: (layer, 0, f)),
                  pl.BlockSpec((None, tf, d), lambda i, f: (layer, f, 0)),
                  pl.BlockSpec((1, d), lambda i, f: (0, 0))],
        out_specs=pl.BlockSpec((tm, d), lambda i, f: (i, 0)),
        out_shape=jax.ShapeDtypeStruct((t, d), F32),
        scratch_shapes=[pltpu.VMEM((tm, d), BF16), pltpu.VMEM((tm, d), F32)],
        compiler_params=_params("parallel", "arbitrary"),
        name="mlp",
    )(x, g_mlp, w_up, w_down, g_final)


def _split_w_in(w_in, widths):
    sb_w, df_w, mb_w, gate_w = widths
    o_df = 3 * sb_w
    o_mb = o_df + 3 * df_w
    o_g = o_mb + 3 * mb_w
    w_tok = jnp.concatenate([w_in[..., :sb_w], w_in[..., o_df:o_mb + mb_w], w_in[..., o_g:o_g + gate_w]], axis=-1)
    w_feat = jnp.concatenate([w_in[..., sb_w:o_df], w_in[..., o_mb + mb_w:o_g]], axis=-1)
    return w_tok.astype(BF16), jnp.swapaxes(w_feat, 1, 2).astype(BF16)


def _trunk(x, tile_pos, tm, seq_tiles, caches, page_table, weights, widths):
    (g_mix, w_tok, w_feat, b_gate, diff_lambda, diff_subln_g, w_br_sb, w_br_diff, w_br_moba, w_out,
     g_mlp, w_up, w_down, g_final) = weights
    t, d = x.shape
    depth = w_tok.shape[0]
    sb_w, df_w, mb_w, _ = widths
    n_seq = t // (tm * seq_tiles)
    s = tm * seq_tiles
    assert t % tm == 0 and t % min(MLP_ROW_TILE, t) == 0 and w_up.shape[-1] % MLP_FF_TILE == 0
    tables = _rope_tables(tile_pos)
    if caches is not None:
        n_pages = page_table.shape[1]
        pt_flat = page_table.reshape(-1)
    new_sb, new_df, new_mb = [], [], []
    for layer in range(depth):
        (q_sb, q_df, q_mb, k_df_bf, v_df_bf, kv_sb_bf, kv_mb_bf, kv_sb, kv_df, kv_mb, gates, kmean) = _in_projection(
            x, g_mix, w_tok, w_feat, layer, tables, widths, tm, seq_tiles)
        if caches is None:
            r3 = lambda a: a.reshape(n_seq, s, a.shape[-1])
            o_sb = _prompt_attention("sb", r3(q_sb), kv_sb_bf, layer, None)
            o_df = _prompt_attention("diff", r3(q_df), (r3(k_df_bf), r3(v_df_bf)), layer, (diff_lambda, diff_subln_g))
            o_mb = _prompt_attention("moba", r3(q_mb), kv_mb_bf, layer, kmean.reshape(n_seq, s // MOBA_BLOCK, mb_w))
        else:
            r4 = lambda a: a.astype(F32).reshape(n_seq, tm, 1, a.shape[-1])
            tok_major = lambda kv_t: jnp.swapaxes(kv_t, 1, 2).astype(F32).reshape(n_seq, tm, 1, kv_t.shape[1])
            o_sb = _sample_attention("sb", r4(q_sb), (tok_major(kv_sb_bf),), caches[0], pt_flat, n_pages, layer, None)
            o_df = _sample_attention("diff", r4(q_df), (r4(k_df_bf), r4(v_df_bf)), caches[1], pt_flat, n_pages, layer,
                                     (diff_lambda, diff_subln_g))
            o_mb = _sample_attention("moba", r4(q_mb), (tok_major(kv_mb_bf),), caches[2], pt_flat, n_pages, layer, None)
        x = _merge(x, o_sb.reshape(t, sb_w), o_df.reshape(t, df_w), o_mb.reshape(t, mb_w), gates,
                   b_gate, w_br_sb, w_br_diff, w_br_moba, w_out, layer, tm)
        x = _mlp(x, g_mlp, w_up, w_down, g_final, layer, layer == depth - 1)
        new_sb.append(kv_sb)
        new_df.append(kv_df)
        new_mb.append(kv_mb)
    return x, jnp.stack(new_sb), jnp.stack(new_df), jnp.stack(new_mb)


def kernel(x_prompt, x_sample, cache_sb_kv, cache_diff_kv, cache_moba_kv, page_table, g_mix, w_in, b_gate, diff_lambda, diff_subln_g, w_br_sb, w_br_diff, w_br_moba, w_out, g_mlp, w_up, w_down, g_final):
    depth, d = w_in.shape[0], w_in.shape[1]
    bp, sp = x_prompt.shape[:2]
    bs, ss = x_sample.shape[:2]
    page = cache_sb_kv.shape[2]
    past_len = page_table.shape[1] * page
    widths = (w_br_sb.shape[1], w_br_diff.shape[1], w_br_moba.shape[1], N_BRANCHES * d)
    assert sp % ATTN_TILE == 0 and ATTN_TILE == MOBA_BLOCK == PROMPT_ROW_TILE
    assert past_len % MOBA_BLOCK == 0 and ss <= MOBA_BLOCK and DEC_CHUNK == MOBA_BLOCK and bs % 8 == 0

    bf = lambda a: a.astype(BF16)
    row = lambda a: a.reshape(depth, 1, a.shape[-1])
    w_tok, w_feat = _split_w_in(w_in, widths)
    weights = (row(g_mix), w_tok, w_feat, b_gate, diff_lambda, row(diff_subln_g), bf(w_br_sb), bf(w_br_diff),
               bf(w_br_moba), bf(w_out), row(g_mlp), bf(w_up), bf(w_down), g_final.reshape(1, -1))

    feat_major = lambda c: jnp.transpose(c, (0, 1, 3, 4, 5, 2)).reshape(c.shape[:2] + (-1, c.shape[2]))
    caches = (feat_major(cache_sb_kv), cache_diff_kv.reshape(cache_diff_kv.shape[:2] + (-1, cache_diff_kv.shape[-1])),
              feat_major(cache_moba_kv))

    pos_p = jnp.arange(sp, dtype=jnp.int32)
    y_p, sb_p, df_p, mb_p = _trunk(x_prompt.reshape(bp * sp, d), pos_p, PROMPT_ROW_TILE, sp // PROMPT_ROW_TILE,
                                   None, None, weights, widths)
    pos_s = jnp.repeat(past_len + jnp.arange(ss, dtype=jnp.int32), bs)
    x_s = jnp.swapaxes(x_sample, 0, 1).reshape(ss * bs, d)
    y_s, sb_s, df_s, mb_s = _trunk(x_s, pos_s, bs, 1, caches, page_table, weights, widths)

    def kv_prompt(kv_t, ref_cache):
        a = kv_t.reshape((depth, bp) + ref_cache.shape[3:] + (sp,))
        return jnp.transpose(a, (0, 1, 5, 2, 3, 4))

    def kv_sample(kv_t, ref_cache):
        a = kv_t.reshape((depth, ss) + ref_cache.shape[3:] + (bs,))
        return jnp.transpose(a, (0, 5, 1, 2, 3, 4))

    df_shape = cache_diff_kv.shape[3:]
    return (y_p.reshape(bp, sp, d), jnp.swapaxes(y_s.reshape(ss, bs, d), 0, 1),
            kv_prompt(sb_p, cache_sb_kv), kv_sample(sb_s, cache_sb_kv),
            df_p.reshape((depth, bp, sp) + df_shape),
            jnp.swapaxes(df_s.reshape((depth, ss, bs) + df_shape), 1, 2),
            kv_prompt(mb_p, cache_moba_kv), kv_sample(mb_s, cache_moba_kv))
```

```python
import functools
import math

import jax
import jax.numpy as jnp
from jax import lax
from jax.experimental import pallas as pl
from jax.experimental.pallas import tpu as pltpu

F32 = jnp.float32
BF16 = jnp.bfloat16

HEAD_DIM = 64
ROPE_THETA = 10000.0
NORM_EPS = 1e-6
MOBA_BLOCK = 256
MOBA_TOPK = 3
N_BRANCHES = 3

LANES = 128
VMEM_LIMIT_BYTES = 56 * 1024 * 1024

PROMPT_ROW_TILE = 256
ATTN_TILE = 256
MLP_ROW_TILE = 512
MLP_FF_TILE = 1024
DEC_CHUNK = 256

Q_SCALE = math.log2(math.e) * HEAD_DIM ** -0.5
MASKED = -1e30


def _lambda_init(layer):
    return 0.8 - 0.6 * math.exp(-0.3 * layer)


def _params(*semantics):
    return pltpu.CompilerParams(dimension_semantics=semantics, vmem_limit_bytes=VMEM_LIMIT_BYTES)


def _dot(a, b):
    return jnp.dot(a, b, preferred_element_type=F32)


def _dot_nt(a, b, precision=None):
    return lax.dot_general(a, b, (((1,), (1,)), ((), ())), preferred_element_type=F32, precision=precision)


def _rms(x, g):
    return (x * lax.rsqrt(jnp.mean(x * x, axis=-1, keepdims=True) + NORM_EPS)) * g


def _strict_upper(n):
    r = lax.broadcasted_iota(jnp.int32, (n, n), 0)
    c = lax.broadcasted_iota(jnp.int32, (n, n), 1)
    return (r > c).astype(BF16)


def _sb_log_rem(z, past):
    nz = -z
    log_rem = jnp.minimum(nz, 0.0) - jnp.log2(1.0 + jnp.exp2(jnp.minimum(z, nz)))
    return log_rem if past is None else jnp.where(past, log_rem, 0.0)


def _sb_weights(z, log_rem, later, past):
    a = jnp.exp2(z + log_rem + later)
    if past is not None:
        a = jnp.where(past, a, 0.0)
    return a.astype(BF16), later[:, 0:1] + log_rem[:, 0:1]


def _row_max(sc):
    return jnp.max(sc, axis=-1, keepdims=True)


def _rope_tables(pos):
    half = HEAD_DIM // 2
    inv_freq = ROPE_THETA ** (-jnp.arange(half, dtype=F32) / half)
    ang = pos.astype(F32)[:, None] * inv_freq[None, :]
    cos, sin = jnp.cos(ang), jnp.sin(ang)
    reps = LANES // HEAD_DIM
    return (jnp.tile(jnp.concatenate([cos, cos], -1), (1, reps)),
            jnp.tile(jnp.concatenate([-sin, sin], -1), (1, reps)), cos.T, sin.T)


def _rope(v, cos, sin_signed):
    half = HEAD_DIM // 2
    first = (lax.broadcasted_iota(jnp.int32, (v.shape[0], LANES), 1) % HEAD_DIM) < half
    outs = []
    for c in range(v.shape[1] // LANES):
        s = v[:, c * LANES:(c + 1) * LANES]
        swapped = jnp.where(first, pltpu.roll(s, LANES - half, 1), pltpu.roll(s, half, 1))
        outs.append(s * cos + swapped * sin_signed)
    return outs[0] if len(outs) == 1 else jnp.concatenate(outs, axis=-1)


def _rope_feature_major(vt, cos_t, sin_t):
    half = HEAD_DIM // 2
    outs = []
    for h in range(vt.shape[0] // HEAD_DIM):
        x1 = vt[h * HEAD_DIM:h * HEAD_DIM + half]
        x2 = vt[h * HEAD_DIM + half:(h + 1) * HEAD_DIM]
        outs += [x1 * cos_t - x2 * sin_t, x2 * cos_t + x1 * sin_t]
    return jnp.concatenate(outs, axis=0)


def _proj_kernel(x_ref, g_ref, w_ref, wt_ref, cos_ref, sin_ref, cost_ref, sint_ref,
                 qsb_ref, qdf_ref, qmb_ref, kdf_bf_ref, vdf_bf_ref, kvsb_bf_ref, kvmb_bf_ref,
                 kvsb_ref, kvdf_ref, kvmb_ref, kmean_ref,
                 *, sb_w, df_w, mb_w):
    xb = _rms(x_ref[...], g_ref[...]).astype(BF16)
    cos, sin = cos_ref[...], sin_ref[...]

    def mm(lo, width):
        return _dot(xb, w_ref[:, lo:lo + width])

    o = 0
    qsb_ref[...] = (mm(o, sb_w) * Q_SCALE).astype(BF16)
    o += sb_w
    qdf_ref[...] = (_rope(mm(o, df_w), cos, sin) * Q_SCALE).astype(BF16)
    o += df_w
    k_df = _rope(mm(o, df_w), cos, sin)
    o += df_w
    v_df = mm(o, df_w)
    o += df_w
    kdf_bf_ref[...] = k_df.astype(BF16)
    vdf_bf_ref[...] = v_df.astype(BF16)
    groups = df_w // LANES
    for j in range(groups):
        kvdf_ref[:, j, :] = k_df[:, j * LANES:(j + 1) * LANES]
        kvdf_ref[:, groups + j, :] = v_df[:, j * LANES:(j + 1) * LANES]
    qmb_ref[...] = (_rope(mm(o, mb_w), cos, sin) * Q_SCALE).astype(BF16)

    kv_t = _dot_nt(wt_ref[...], xb)
    kv_sb = kv_t[:2 * sb_w]
    kvsb_ref[...] = kv_sb
    kvsb_bf_ref[...] = kv_sb.astype(BF16)
    k_mb = _rope_feature_major(kv_t[2 * sb_w:2 * sb_w + mb_w], cost_ref[...], sint_ref[...])
    v_mb = kv_t[2 * sb_w + mb_w:]
    kvmb_ref[:mb_w] = k_mb
    kvmb_ref[mb_w:] = v_mb
    kvmb_bf_ref[:mb_w] = k_mb.astype(BF16)
    kvmb_bf_ref[mb_w:] = v_mb.astype(BF16)
    kmean_ref[...] = jnp.mean(k_mb, axis=-1, keepdims=True)


def _in_projection(x, g, w_tok, w_feat, layer, tables, widths, tm, seq_tiles):
    t, d = x.shape
    sb_w, df_w, mb_w, _ = widths
    cos, sin, cos_t, sin_t = tables
    tab_tiles = cos.shape[0] // tm
    n_tiles = t // tm
    n_seq = n_tiles // seq_tiles
    s = seq_tiles * tm
    half = HEAD_DIM // 2
    kv_groups = 2 * df_w // LANES

    def rows(w):
        return pl.BlockSpec((tm, w), lambda i: (i, 0))

    def feat(w):
        return pl.BlockSpec((None, w, tm), lambda i: (i // seq_tiles, 0, i % seq_tiles))

    out_shapes = [
        jax.ShapeDtypeStruct((t, sb_w), BF16), jax.ShapeDtypeStruct((t, df_w), BF16),
        jax.ShapeDtypeStruct((t, mb_w), BF16),
        jax.ShapeDtypeStruct((t, df_w), BF16), jax.ShapeDtypeStruct((t, df_w), BF16),
        jax.ShapeDtypeStruct((n_seq, 2 * sb_w, s), BF16), jax.ShapeDtypeStruct((n_seq, 2 * mb_w, s), BF16),
        jax.ShapeDtypeStruct((n_seq, 2 * sb_w, s), F32),
        jax.ShapeDtypeStruct((t, kv_groups, LANES), F32),
        jax.ShapeDtypeStruct((n_seq, 2 * mb_w, s), F32),
        jax.ShapeDtypeStruct((n_tiles, mb_w, 1), F32),
    ]
    out_specs = [rows(sb_w), rows(df_w), rows(mb_w), rows(df_w), rows(df_w), feat(2 * sb_w), feat(2 * mb_w),
                 feat(2 * sb_w), pl.BlockSpec((tm, kv_groups, LANES), lambda i: (i, 0, 0)), feat(2 * mb_w),
                 pl.BlockSpec((None, mb_w, 1), lambda i: (i, 0, 0))]
    return pl.pallas_call(
        functools.partial(_proj_kernel, sb_w=sb_w, df_w=df_w, mb_w=mb_w),
        grid=(n_tiles,),
        in_specs=[rows(d),
                  pl.BlockSpec((None, 1, d), lambda i: (layer, 0, 0)),
                  pl.BlockSpec((None, d, w_tok.shape[-1]), lambda i: (layer, 0, 0)),
                  pl.BlockSpec((None, w_feat.shape[1], d), lambda i: (layer, 0, 0)),
                  pl.BlockSpec((tm, LANES), lambda i: (i % tab_tiles, 0)),
                  pl.BlockSpec((tm, LANES), lambda i: (i % tab_tiles, 0)),
                  pl.BlockSpec((half, tm), lambda i: (0, i % tab_tiles)),
                  pl.BlockSpec((half, tm), lambda i: (0, i % tab_tiles))],
        out_specs=out_specs,
        out_shape=out_shapes,
        compiler_params=_params("parallel"),
        name="in_projection",
    )(x, g, w_tok, w_feat, cos, sin, cos_t, sin_t)


def _causal_pairs(n):
    qi = [i for i in range(n) for _ in range(i + 1)]
    kj = [i - j for i in range(n) for j in range(i + 1)]
    return jnp.asarray(qi, jnp.int32), jnp.asarray(kj, jnp.int32)


def _diag_mask(t, strict):
    r = lax.broadcasted_iota(jnp.int32, (t, t), 0)
    c = lax.broadcasted_iota(jnp.int32, (t, t), 1)
    return (c < r) if strict else (c <= r)


def _sb_prompt_kernel(qi_ref, kj_ref, q_ref, kt_ref, vt_ref, o_ref, acc_ref, carry_ref, *, heads):
    s = pl.program_id(1)
    i, j = qi_ref[s], kj_ref[s]
    t = q_ref.shape[0]

    def sweep(past):
        upper = _strict_upper(t)
        hs = range(heads)
        cols = [slice(h * HEAD_DIM, (h + 1) * HEAD_DIM) for h in hs]
        zs = [_dot(q_ref[:, cols[h]], kt_ref[cols[h], :]) for h in hs]
        lrs = [_sb_log_rem(zs[h], past) for h in hs]
        laters = [_dot(lrs[h].astype(BF16), upper) for h in hs]
        weights = [_sb_weights(zs[h], lrs[h], laters[h], past) for h in hs]
        outs = [_dot_nt(weights[h][0], vt_ref[cols[h], :]) for h in hs]
        for h in hs:
            carry = carry_ref[h]
            acc_ref[:, cols[h]] += jnp.exp2(carry) * outs[h]
            carry_ref[h] = carry + weights[h][1]

    @pl.when(j == i)
    def _():
        acc_ref[...] = jnp.zeros_like(acc_ref)
        carry_ref[...] = jnp.zeros_like(carry_ref)
        sweep(_diag_mask(t, True))

    @pl.when(j != i)
    def _():
        sweep(None)

    @pl.when(j == 0)
    def _():
        o_ref[...] = acc_ref[...]


def _diff_lambda(lp, lam_init):
    a = jnp.sum(lp[0:1] * lp[1:2], axis=-1, keepdims=True)
    b = jnp.sum(lp[2:3] * lp[3:4], axis=-1, keepdims=True)
    return jnp.exp(a) - jnp.exp(b) + lam_init


def _diff_prompt_kernel(qi_ref, kj_ref, q_ref, k_ref, v_ref, lp_ref, g_ref, o_ref, acc_ref, m_ref,
                        *, heads, lam_init):
    s = pl.program_id(1)
    i, j = qi_ref[s], kj_ref[s]
    t = q_ref.shape[0]
    vw = 2 * HEAD_DIM

    def sweep(causal):
        idxs = [(h, c) for h in range(heads) for c in range(2)]
        scs = []
        for h, c in idxs:
            cols = slice(h * vw + c * HEAD_DIM, h * vw + (c + 1) * HEAD_DIM)
            sc = _dot_nt(q_ref[:, cols], k_ref[:, cols])
            if causal is not None:
                sc = jnp.where(causal, sc, -jnp.inf)
            scs.append(sc)
        alphas, ps = [], []
        for n in range(len(idxs)):
            m_old = m_ref[n]
            m_new = jnp.maximum(m_old, _row_max(scs[n]))
            alphas.append(jnp.exp2(m_old - m_new))
            ps.append(jnp.exp2(scs[n] - jnp.concatenate([m_new] * (t // LANES), axis=1)).astype(BF16))
            m_ref[n] = m_new
        ones = jnp.ones((t, vw), BF16)
        pvs = [_dot(jnp.concatenate(ps[2 * h:2 * h + 2], axis=0),
                    jnp.concatenate([v_ref[:, h * vw:(h + 1) * vw], ones], axis=1)) for h in range(heads)]
        for h in range(heads):
            alpha = jnp.concatenate(alphas[2 * h:2 * h + 2], axis=0)
            acc_ref[h] = jnp.concatenate([alpha] * (2 * vw // LANES), axis=1) * acc_ref[h] + pvs[h]

    @pl.when(j == i)
    def _():
        acc_ref[...] = jnp.zeros_like(acc_ref)
        m_ref[...] = jnp.full_like(m_ref, -jnp.inf)
        sweep(_diag_mask(t, False))

    @pl.when(j != i)
    def _():
        sweep(None)

    @pl.when(j == 0)
    def _():
        lam = _diff_lambda(lp_ref[...], lam_init)
        for h in range(heads):
            o1 = acc_ref[h, 0:t, 0:vw] / acc_ref[h, 0:t, vw:vw + 1]
            o2 = acc_ref[h, t:2 * t, 0:vw] / acc_ref[h, t:2 * t, vw:vw + 1]
            o_ref[:, h * vw:(h + 1) * vw] = _rms(o1 - lam * o2, g_ref[...]) * (1.0 - lam_init)


def _moba_prompt_kernel(qi_ref, kj_ref, q_ref, kt_ref, vt_ref, kmean_ref, o_ref, acc_ref, m_ref, qaug_ref,
                        *, heads, n_blk):
    s = pl.program_id(1)
    i, j = qi_ref[s], kj_ref[s]
    t = q_ref.shape[0]
    gate_rows = ((n_blk + 7) // 8) * 8

    def select_blocks():
        lane = lax.broadcasted_iota(jnp.int32, (t, LANES), 1)
        blk = lax.broadcasted_iota(jnp.int32, (gate_rows, t), 0)
        for h in range(heads):
            qh = q_ref[:, h * HEAD_DIM:(h + 1) * HEAD_DIM].astype(F32)
            gate = _dot_nt(kmean_ref[h], qh, precision=lax.Precision.HIGHEST)[HEAD_DIM:HEAD_DIM + gate_rows]
            rank = jnp.zeros((gate_rows, t), F32)
            for m in range(n_blk):
                gm = gate[m:m + 1, :]
                beats = jnp.logical_or(gm > gate, jnp.logical_and(gm == gate, blk > m))
                rank = rank + jnp.where(beats, 1.0, 0.0) * jnp.where(m < i, 1.0, 0.0)
            attended = jnp.logical_or(jnp.logical_and(rank < MOBA_TOPK, blk < i), blk == i)
            att = jnp.concatenate([jnp.zeros((HEAD_DIM, t), F32), jnp.where(attended, 1.0, 0.0),
                                   jnp.zeros((LANES - HEAD_DIM - gate_rows, t), F32)], axis=0).T
            pair = q_ref[:, (h // 2) * LANES:(h // 2 + 1) * LANES].astype(F32)
            if h % 2:
                pair = pltpu.roll(pair, HEAD_DIM, 1)
            in_gate_lanes = jnp.logical_and(lane >= HEAD_DIM, lane < HEAD_DIM + n_blk)
            pen = jnp.where(jnp.logical_and(in_gate_lanes, att < 0.5), MASKED, 0.0)
            qaug_ref[h] = jnp.where(lane < HEAD_DIM, pair, pen).astype(BF16)

    def sweep(causal):
        rows = lax.broadcasted_iota(jnp.int32, (LANES - HEAD_DIM, t), 0)
        one_hot = jnp.where(rows == j, 1.0, 0.0).astype(BF16)
        ones = jnp.ones((LANES - HEAD_DIM, t), BF16)
        hs = range(heads)
        cols = [slice(h * HEAD_DIM, (h + 1) * HEAD_DIM) for h in hs]
        scs = []
        for h in hs:
            sc = _dot(qaug_ref[h], jnp.concatenate([kt_ref[cols[h], :], one_hot], axis=0))
            if causal is not None:
                sc = jnp.where(causal, sc, -jnp.inf)
            scs.append(sc)
        alphas, ps = [], []
        for h in hs:
            m_old = m_ref[h]
            m_new = jnp.maximum(m_old, _row_max(scs[h]))
            alphas.append(jnp.exp2(m_old - m_new))
            ps.append(jnp.exp2(scs[h] - jnp.concatenate([m_new] * (t // LANES), axis=1)).astype(BF16))
            m_ref[h] = m_new
        pvs = [_dot_nt(ps[h], jnp.concatenate([vt_ref[cols[h], :], ones], axis=0)) for h in hs]
        for h in hs:
            acc_ref[h] = alphas[h] * acc_ref[h] + pvs[h]

    @pl.when(j == i)
    def _():
        acc_ref[...] = jnp.zeros_like(acc_ref)
        m_ref[...] = jnp.full_like(m_ref, -jnp.inf)
        select_blocks()
        sweep(_diag_mask(t, False))

    @pl.when(j != i)
    def _():
        sweep(None)

    @pl.when(j == 0)
    def _():
        for h in range(heads):
            o_ref[:, h * HEAD_DIM:(h + 1) * HEAD_DIM] = acc_ref[h, :, 0:HEAD_DIM] / acc_ref[h, :, HEAD_DIM:HEAD_DIM + 1]


def _prompt_branch(kind, q, kv, layer, extra):
    b, s, w = q.shape
    t = ATTN_TILE
    n = s // t
    q_spec = pl.BlockSpec((None, t, w), lambda bb, st, qi, kj: (bb, qi[st], 0))
    kt_spec = pl.BlockSpec((None, w, t), lambda bb, st, qi, kj: (bb, 0, kj[st]))
    vt_spec = pl.BlockSpec((None, w, t), lambda bb, st, qi, kj: (bb, 1, kj[st]))
    if kind == "sb":
        heads = w // HEAD_DIM
        body = functools.partial(_sb_prompt_kernel, heads=heads)
        in_specs, args = [q_spec, kt_spec, vt_spec], (q, kv, kv)
        scratch = [pltpu.VMEM((t, w), F32), pltpu.VMEM((heads, t, 1), F32)]
    elif kind == "diff":
        heads = w // (2 * HEAD_DIM)
        lp, g = extra
        body = functools.partial(_diff_prompt_kernel, heads=heads, lam_init=_lambda_init(layer))
        tok_spec = pl.BlockSpec((None, t, w), lambda bb, st, qi, kj: (bb, kj[st], 0))
        in_specs = [q_spec, tok_spec, tok_spec,
                    pl.BlockSpec((None, 4, HEAD_DIM), lambda bb, st, qi, kj: (layer, 0, 0)),
                    pl.BlockSpec((None, 1, 2 * HEAD_DIM), lambda bb, st, qi, kj: (layer, 0, 0))]
        args = (q, kv[0], kv[1], lp, g)
        scratch = [pltpu.VMEM((heads, 2 * t, 4 * HEAD_DIM), F32), pltpu.VMEM((2 * heads, t, LANES), F32)]
    else:
        heads = w // HEAD_DIM
        assert HEAD_DIM + n <= LANES
        kmean = extra.reshape(b, n, heads, HEAD_DIM).transpose(0, 2, 1, 3)
        kmean = jnp.pad(kmean, ((0, 0), (0, 0), (HEAD_DIM, LANES - HEAD_DIM - n), (0, 0)))
        body = functools.partial(_moba_prompt_kernel, heads=heads, n_blk=n)
        in_specs = [q_spec, kt_spec, vt_spec,
                    pl.BlockSpec((None, heads, LANES, HEAD_DIM), lambda bb, st, qi, kj: (bb, 0, 0, 0))]
        args = (q, kv, kv, kmean)
        scratch = [pltpu.VMEM((heads, t, LANES), F32), pltpu.VMEM((heads, t, LANES), F32),
                   pltpu.VMEM((heads, t, LANES), BF16)]
    return body, in_specs, args, scratch


def _prompt_attention(branches):
    parts = [_prompt_branch(*br) for br in branches]
    qs = [br[1] for br in branches]
    b, s, _ = qs[0].shape
    t = ATTN_TILE
    qi, kj = _causal_pairs(s // t)
    n_in = [len(p[1]) for p in parts]
    n_scr = [len(p[3]) for p in parts]

    def body(qi_ref, kj_ref, *refs):
        ins, outs, scr = refs[:sum(n_in)], refs[sum(n_in):sum(n_in) + len(parts)], refs[sum(n_in) + len(parts):]
        for k, part in enumerate(parts):
            i0, s0 = sum(n_in[:k]), sum(n_scr[:k])
            part[0](qi_ref, kj_ref, *ins[i0:i0 + n_in[k]], outs[k], *scr[s0:s0 + n_scr[k]])

    return pl.pallas_call(
        body,
        grid_spec=pltpu.PrefetchScalarGridSpec(
            num_scalar_prefetch=2, grid=(b, int(qi.shape[0])),
            in_specs=[sp for p in parts for sp in p[1]],
            out_specs=[pl.BlockSpec((None, t, q.shape[-1]), lambda bb, st, qi, kj: (bb, qi[st], 0)) for q in qs],
            scratch_shapes=[sc for p in parts for sc in p[3]]),
        out_shape=[jax.ShapeDtypeStruct(q.shape, F32) for q in qs],
        compiler_params=_params("parallel", "arbitrary"),
        name="_".join(br[0] for br in branches) + "_prompt_attention",
    )(qi, kj, *[a for p in parts for a in p[2]])


def _block_diag_queries(q, groups, group_w):
    nq, w = q.shape
    rows = lax.broadcasted_iota(jnp.int32, (groups * nq, w), 0) // nq
    cols = lax.broadcasted_iota(jnp.int32, (groups * nq, w), 1) // group_w
    return jnp.where(rows == cols, jnp.concatenate([q] * groups, axis=0), 0.0)


def _head_diagonal(full, heads, nq, head_w):
    cols = lax.broadcasted_iota(jnp.int32, (nq, heads * head_w), 1) // head_w
    out = jnp.zeros((nq, heads * head_w), F32)
    for h in range(heads):
        out = jnp.where(cols == h, full[h * nq:(h + 1) * nq], out)
    return out


def _stage_new_tokens(pad_ref, kv_new):
    pad_ref[...] = jnp.zeros_like(pad_ref)
    pad_ref[0:kv_new.shape[0], :] = kv_new.astype(pad_ref.dtype)


def _feature_major_chunks(pad_ref, page_refs, ppc, w):
    yield True, pad_ref[:, :w].astype(BF16), pad_ref[:, w:].astype(BF16)
    for c in reversed(range(len(page_refs) // ppc)):
        refs = page_refs[c * ppc:(c + 1) * ppc]
        yield (False, jnp.concatenate([r[:w, :] for r in refs], axis=1).astype(BF16),
               jnp.concatenate([r[w:, :] for r in refs], axis=1).astype(BF16))


def _new_token_masks(rows, nq, ck):
    r = lax.broadcasted_iota(jnp.int32, (rows, ck), 0) % nq
    c = lax.broadcasted_iota(jnp.int32, (rows, ck), 1)
    return c < r, c <= r


def _sb_sample_kernel(pt_ref, q_ref, kvn_ref, *rest, heads, n_pages, ppc):
    page_refs, o_ref, pad_ref = rest[:n_pages], rest[n_pages], rest[n_pages + 1]
    nq, w = q_ref.shape[0], q_ref.shape[-1]
    ck = pad_ref.shape[0]
    _stage_new_tokens(pad_ref, kvn_ref[:, 0, :])
    qbd = _block_diag_queries(q_ref[:, 0, :], heads, HEAD_DIM).astype(BF16)
    past_new, _ = _new_token_masks(heads * nq, nq, ck)
    upper = _strict_upper(ck)
    chunks = list(_feature_major_chunks(pad_ref, page_refs, ppc, w))
    masks = [past_new if is_new else None for is_new, _, _ in chunks]
    n = range(len(chunks))
    zs = [_dot_nt(qbd, k) if is_new else _dot(qbd, k) for is_new, k, _ in chunks]
    lrs = [_sb_log_rem(zs[c], masks[c]) for c in n]
    laters = [_dot(lrs[c].astype(BF16), upper) for c in n]
    weights = [_sb_weights(zs[c], lrs[c], laters[c], masks[c]) for c in n]
    outs = [_dot(weights[c][0], chunks[c][2]) if chunks[c][0] else _dot_nt(weights[c][0], chunks[c][2]) for c in n]
    acc = jnp.zeros((heads * nq, w), F32)
    carry = jnp.zeros((heads * nq, 1), F32)
    for c in n:
        acc = acc + jnp.exp2(carry) * outs[c]
        carry = carry + weights[c][1]
    o_ref[:, 0, :] = _head_diagonal(acc, heads, nq, HEAD_DIM)


def _softmax_over_chunks(scs, values, token_major):
    m = functools.reduce(jnp.maximum, [_row_max(sc) for sc in scs])
    ps = [jnp.exp2(sc - m) for sc in scs]
    l = sum(jnp.sum(p, axis=-1, keepdims=True) for p in ps)
    outs = [_dot(p.astype(BF16), v) if tm else _dot_nt(p.astype(BF16), v) for p, v, tm in zip(ps, values, token_major)]
    return sum(outs) / l


def _moba_sample_kernel(pt_ref, q_ref, kvn_ref, *rest, heads, n_pages, ppc):
    page_refs, o_ref, pad_ref = rest[:n_pages], rest[n_pages], rest[n_pages + 1]
    nq, w = q_ref.shape[0], q_ref.shape[-1]
    ck = pad_ref.shape[0]
    _stage_new_tokens(pad_ref, kvn_ref[:, 0, :])
    qf = _block_diag_queries(q_ref[:, 0, :].astype(F32), heads, HEAD_DIM)
    qbd = qf.astype(BF16)
    rows = heads * nq
    _, causal_new = _new_token_masks(rows, nq, ck)
    n_blk = n_pages // ppc
    lane = lax.broadcasted_iota(jnp.int32, (w, LANES), 1)
    kmean = jnp.zeros((w, LANES), F32)
    for b in range(n_blk):
        k_sum = sum(jnp.sum(page_refs[b * ppc + p][:w, :], axis=-1, keepdims=True) for p in range(ppc))
        kmean = kmean + jnp.where(lane == b, k_sum * (1.0 / ck), 0.0)
    gate = jnp.dot(qf, kmean, preferred_element_type=F32, precision=lax.Precision.HIGHEST)
    blk_lane = lax.broadcasted_iota(jnp.int32, (rows, LANES), 1)
    rank = jnp.zeros((rows, LANES), F32)
    for m in range(n_blk):
        gm = gate[:, m:m + 1]
        beats = jnp.logical_or(gm > gate, jnp.logical_and(gm == gate, blk_lane > m))
        rank = rank + jnp.where(beats, 1.0, 0.0)
    pen = jnp.where(rank < MOBA_TOPK, 0.0, MASKED)
    chunks = list(_feature_major_chunks(pad_ref, page_refs, ppc, w))
    scs = []
    for c, (is_new, k, _) in enumerate(chunks):
        if is_new:
            scs.append(jnp.where(causal_new, _dot_nt(qbd, k), -jnp.inf))
        else:
            blk = n_blk - c
            scs.append(_dot(qbd, k) + pen[:, blk:blk + 1])
    o = _softmax_over_chunks(scs, [v for _, _, v in chunks], [is_new for is_new, _, _ in chunks])
    o_ref[:, 0, :] = _head_diagonal(o, heads, nq, HEAD_DIM)


def _diff_sample_kernel(pt_ref, q_ref, kn_ref, vn_ref, lp_ref, g_ref, *rest, heads, n_pages, ppc, lam_init):
    page_refs, o_ref, pad_ref = rest[:n_pages], rest[n_pages], rest[n_pages + 1]
    nq, w = q_ref.shape[0], q_ref.shape[-1]
    ck = pad_ref.shape[0]
    vw = 2 * HEAD_DIM
    groups = 2 * heads
    page = page_refs[0].shape[0] // groups

    def token_major(ref, first):
        return jnp.concatenate([ref[pl.ds(first + h, page, stride=groups), :] for h in range(heads)], axis=-1)

    _stage_new_tokens(pad_ref, jnp.concatenate([kn_ref[:, 0, :], vn_ref[:, 0, :]], axis=-1))
    qg = _block_diag_queries(q_ref[:, 0, :].astype(F32), groups, HEAD_DIM)
    qbd = jnp.concatenate([qg[(2 * h + c) * nq:(2 * h + c + 1) * nq] for c in range(2) for h in range(heads)],
                          axis=0).astype(BF16)
    rows = groups * nq
    _, causal_new = _new_token_masks(rows, nq, ck)
    scs = [jnp.where(causal_new, _dot_nt(qbd, pad_ref[:, :w].astype(BF16)), -jnp.inf)]
    values = [pad_ref[:, w:].astype(BF16)]
    for c in reversed(range(n_pages // ppc)):
        refs = page_refs[c * ppc:(c + 1) * ppc]
        k = jnp.concatenate([token_major(r, 0) for r in refs], axis=0).astype(BF16)
        scs.append(_dot_nt(qbd, k))
        values.append(jnp.concatenate([token_major(r, heads) for r in refs], axis=0).astype(BF16))
    o = _softmax_over_chunks(scs, values, [True] * len(scs))
    half = heads * nq
    lam = _diff_lambda(lp_ref[...], lam_init)
    od = _head_diagonal(o[:half] - lam * o[half:], heads, nq, vw)
    for h in range(heads):
        o_ref[:, 0, h * vw:(h + 1) * vw] = _rms(od[:, h * vw:(h + 1) * vw], g_ref[...]) * (1.0 - lam_init)


def _sample_attention(kind, q, kv_new, cache, page_table_flat, n_pages, layer, extra):
    nq, db, _, w = q.shape
    page = cache.shape[3] if kind != "diff" else cache.shape[2] * cache.shape[3] // (2 * w)
    ppc = DEC_CHUNK // page
    assert ppc * page == DEC_CHUNK and n_pages % ppc == 0 and nq <= DEC_CHUNK
    tok = lambda a: pl.BlockSpec((nq, None) + a.shape[2:], lambda b, pt: (0, b, 0, 0))
    page_specs = [pl.BlockSpec((None, None) + cache.shape[2:], lambda b, pt, p=p: (layer, pt[b * n_pages + p], 0, 0))
                  for p in range(n_pages)]
    lead_specs, lead_args = [tok(q)] + [tok(a) for a in kv_new], (q,) + tuple(kv_new)
    if kind == "sb":
        body = functools.partial(_sb_sample_kernel, heads=w // HEAD_DIM, n_pages=n_pages, ppc=ppc)
    elif kind == "moba":
        body = functools.partial(_moba_sample_kernel, heads=w // HEAD_DIM, n_pages=n_pages, ppc=ppc)
    else:
        lp, g = extra
        body = functools.partial(_diff_sample_kernel, heads=w // (2 * HEAD_DIM), n_pages=n_pages, ppc=ppc,
                                 lam_init=_lambda_init(layer))
        lead_specs += [pl.BlockSpec((None, 4, HEAD_DIM), lambda b, pt: (layer, 0, 0)),
                       pl.BlockSpec((None, 1, 2 * HEAD_DIM), lambda b, pt: (layer, 0, 0))]
        lead_args += (lp, g)
    return pl.pallas_call(
        body,
        grid_spec=pltpu.PrefetchScalarGridSpec(
            num_scalar_prefetch=1, grid=(db,),
            in_specs=lead_specs + page_specs, out_specs=tok(jax.ShapeDtypeStruct(q.shape, F32)),
            scratch_shapes=[pltpu.VMEM((DEC_CHUNK, 2 * w), F32)]),
        out_shape=jax.ShapeDtypeStruct(q.shape, F32),
        compiler_params=_params("parallel"),
        name=kind + "_sample_attention",
    )(page_table_flat, *lead_args, *([cache] * n_pages))


def _merge_kernel(x_ref, g_ref, osb_ref, odf_ref, omb_ref, wg_ref, b_ref, wsb_ref, wdf_ref, wmb_ref, wout_ref, o_ref):
    d = x_ref.shape[1]
    x = x_ref[...]
    xb = _rms(x, g_ref[...]).astype(BF16)
    merged = jnp.zeros(x.shape, F32)
    for n, (o_r, w_r) in enumerate(((osb_ref, wsb_ref), (odf_ref, wdf_ref), (omb_ref, wmb_ref))):
        y = _dot(o_r[...].astype(BF16), w_r[...])
        gate = jax.nn.sigmoid(_dot(xb, wg_ref[:, n * d:(n + 1) * d]) + b_ref[n:n + 1, :])
        merged = merged + gate * y
    o_ref[...] = x + _dot(merged.astype(BF16), wout_ref[...])


def _merge(x, g_mix, o_sb, o_df, o_mb, w_gate, b_gate, w_sb, w_df, w_mb, w_out, layer, tm):
    t, d = x.shape
    rows = lambda w: pl.BlockSpec((tm, w), lambda i: (i, 0))
    lay = lambda a: pl.BlockSpec((None,) + a.shape[1:], lambda i: (layer, 0, 0))
    return pl.pallas_call(
        _merge_kernel,
        grid=(t // tm,),
        in_specs=[rows(d), lay(g_mix), rows(o_sb.shape[1]), rows(o_df.shape[1]), rows(o_mb.shape[1]),
                  lay(w_gate), lay(b_gate), lay(w_sb), lay(w_df), lay(w_mb), lay(w_out)],
        out_specs=rows(d),
        out_shape=jax.ShapeDtypeStruct((t, d), F32),
        compiler_params=_params("parallel"),
        name="merge",
    )(x, g_mix, o_sb, o_df, o_mb, w_gate, b_gate, w_sb, w_df, w_mb, w_out)


def _mlp_kernel(x_ref, g_ref, wup_ref, wdn_ref, gf_ref, o_ref, hn_ref, acc_ref, *, final_norm):
    f = pl.program_id(1)

    @pl.when(f == 0)
    def _():
        hn_ref[...] = _rms(x_ref[...], g_ref[...]).astype(BF16)
        acc_ref[...] = x_ref[...]

    up = _dot(hn_ref[...], wup_ref[...])
    act = jnp.square(jnp.maximum(up, 0.0))
    acc_ref[...] += _dot(act.astype(BF16), wdn_ref[...])

    @pl.when(f == pl.num_programs(1) - 1)
    def _():
        y = acc_ref[...]
        o_ref[...] = _rms(y, gf_ref[...]) if final_norm else y


def _mlp(x, g_mlp, w_up, w_down, g_final, layer, final_norm):
    t, d = x.shape
    ff = w_up.shape[-1]
    tm = min(MLP_ROW_TILE, t)
    tf = MLP_FF_TILE
    return pl.pallas_call(
        functools.partial(_mlp_kernel, final_norm=final_norm),
        grid=(t // tm, ff // tf),
        in_specs=[pl.BlockSpec((tm, d), lambda i, f: (i, 0)),
                  pl.BlockSpec((None, 1, d), lambda i, f: (layer, 0, 0)),
                  pl.BlockSpec((None, d, tf), lambda i, f: (layer, 0, f)),
                  pl.BlockSpec((None, tf, d), lambda i, f: (layer, f, 0)),
                  pl.BlockSpec((1, d), lambda i, f: (0, 0))],
        out_specs=pl.BlockSpec((tm, d), lambda i, f: (i, 0)),
        out_shape=jax.ShapeDtypeStruct((t, d), F32),
        scratch_shapes=[pltpu.VMEM((tm, d), BF16), pltpu.VMEM((tm, d), F32)],
        compiler_params=_params("parallel", "arbitrary"),
        name="mlp",
    )(x, g_mlp, w_up, w_down, g_final)


def _split_w_in(w_in, widths):
    sb_w, df_w, mb_w, gate_w = widths
    o_df = 3 * sb_w
    o_mb = o_df + 3 * df_w
    o_g = o_mb + 3 * mb_w
    w_tok = jnp.concatenate([w_in[..., :sb_w], w_in[..., o_df:o_mb + mb_w]], axis=-1)
    w_feat = jnp.concatenate([w_in[..., sb_w:o_df], w_in[..., o_mb + mb_w:o_g]], axis=-1)
    return w_tok.astype(BF16), jnp.swapaxes(w_feat, 1, 2).astype(BF16), w_in[..., o_g:o_g + gate_w].astype(BF16)


def _trunk(x, tile_pos, tm, seq_tiles, caches, page_table, weights, widths):
    (g_mix, w_tok, w_feat, w_gate, b_gate, diff_lambda, diff_subln_g, w_br_sb, w_br_diff, w_br_moba, w_out,
     g_mlp, w_up, w_down, g_final) = weights
    t, d = x.shape
    depth = w_tok.shape[0]
    sb_w, df_w, mb_w, _ = widths
    n_seq = t // (tm * seq_tiles)
    s = tm * seq_tiles
    assert t % tm == 0 and t % min(MLP_ROW_TILE, t) == 0 and w_up.shape[-1] % MLP_FF_TILE == 0
    tables = _rope_tables(tile_pos)
    if caches is not None:
        n_pages = page_table.shape[1]
        pt_flat = page_table.reshape(-1)
    new_sb, new_df, new_mb = [], [], []
    for layer in range(depth):
        (q_sb, q_df, q_mb, k_df_bf, v_df_bf, kv_sb_bf, kv_mb_bf, kv_sb, kv_df, kv_mb, kmean) = _in_projection(
            x, g_mix, w_tok, w_feat, layer, tables, widths, tm, seq_tiles)
        if caches is None:
            r3 = lambda a: a.reshape(n_seq, s, a.shape[-1])
            o_sb, o_df, o_mb = _prompt_attention([
                ("sb", r3(q_sb), kv_sb_bf, layer, None),
                ("diff", r3(q_df), (r3(k_df_bf), r3(v_df_bf)), layer, (diff_lambda, diff_subln_g)),
                ("moba", r3(q_mb), kv_mb_bf, layer, kmean.reshape(n_seq, s // MOBA_BLOCK, mb_w))])
        else:
            r4 = lambda a: a.astype(F32).reshape(n_seq, tm, 1, a.shape[-1])
            tok_major = lambda kv_t: jnp.swapaxes(kv_t, 1, 2).astype(F32).reshape(n_seq, tm, 1, kv_t.shape[1])
            o_sb = _sample_attention("sb", r4(q_sb), (tok_major(kv_sb_bf),), caches[0], pt_flat, n_pages, layer, None)
            o_df = _sample_attention("diff", r4(q_df), (r4(k_df_bf), r4(v_df_bf)), caches[1], pt_flat, n_pages, layer,
                                     (diff_lambda, diff_subln_g))
            o_mb = _sample_attention("moba", r4(q_mb), (tok_major(kv_mb_bf),), caches[2], pt_flat, n_pages, layer, None)
        x = _merge(x, g_mix, o_sb.reshape(t, sb_w), o_df.reshape(t, df_w), o_mb.reshape(t, mb_w), w_gate,
                   b_gate, w_br_sb, w_br_diff, w_br_moba, w_out, layer, tm)
        x = _mlp(x, g_mlp, w_up, w_down, g_final, layer, layer == depth - 1)
        new_sb.append(kv_sb)
        new_df.append(kv_df)
        new_mb.append(kv_mb)
    return x, jnp.stack(new_sb), jnp.stack(new_df), jnp.stack(new_mb)


def kernel(x_prompt, x_sample, cache_sb_kv, cache_diff_kv, cache_moba_kv, page_table, g_mix, w_in, b_gate, diff_lambda, diff_subln_g, w_br_sb, w_br_diff, w_br_moba, w_out, g_mlp, w_up, w_down, g_final):
    depth, d = w_in.shape[0], w_in.shape[1]
    bp, sp = x_prompt.shape[:2]
    bs, ss = x_sample.shape[:2]
    page = cache_sb_kv.shape[2]
    past_len = page_table.shape[1] * page
    widths = (w_br_sb.shape[1], w_br_diff.shape[1], w_br_moba.shape[1], N_BRANCHES * d)
    assert sp % ATTN_TILE == 0 and ATTN_TILE == MOBA_BLOCK == PROMPT_ROW_TILE
    assert past_len % MOBA_BLOCK == 0 and ss <= MOBA_BLOCK and DEC_CHUNK == MOBA_BLOCK and bs % 8 == 0

    bf = lambda a: a.astype(BF16)
    row = lambda a: a.reshape(depth, 1, a.shape[-1])
    w_tok, w_feat, w_gate = _split_w_in(w_in, widths)
    weights = (row(g_mix), w_tok, w_feat, w_gate, b_gate, diff_lambda, row(diff_subln_g), bf(w_br_sb), bf(w_br_diff),
               bf(w_br_moba), bf(w_out), row(g_mlp), bf(w_up), bf(w_down), g_final.reshape(1, -1))

    feat_major = lambda c: jnp.transpose(c, (0, 1, 3, 4, 5, 2)).reshape(c.shape[:2] + (-1, c.shape[2]))
    caches = (feat_major(cache_sb_kv), cache_diff_kv.reshape(cache_diff_kv.shape[:2] + (-1, cache_diff_kv.shape[-1])),
              feat_major(cache_moba_kv))

    pos_p = jnp.arange(sp, dtype=jnp.int32)
    y_p, sb_p, df_p, mb_p = _trunk(x_prompt.reshape(bp * sp, d), pos_p, PROMPT_ROW_TILE, sp // PROMPT_ROW_TILE,
                                   None, None, weights, widths)
    pos_s = jnp.repeat(past_len + jnp.arange(ss, dtype=jnp.int32), bs)
    x_s = jnp.swapaxes(x_sample, 0, 1).reshape(ss * bs, d)
    y_s, sb_s, df_s, mb_s = _trunk(x_s, pos_s, bs, 1, caches, page_table, weights, widths)

    def kv_prompt(kv_t, ref_cache):
        a = kv_t.reshape((depth, bp) + ref_cache.shape[3:] + (sp,))
        return jnp.transpose(a, (0, 1, 5, 2, 3, 4))

    def kv_sample(kv_t, ref_cache):
        a = kv_t.reshape((depth, ss) + ref_cache.shape[3:] + (bs,))
        return jnp.transpose(a, (0, 5, 1, 2, 3, 4))

    df_shape = cache_diff_kv.shape[3:]
    return (y_p.reshape(bp, sp, d), jnp.swapaxes(y_s.reshape(ss, bs, d), 0, 1),
            kv_prompt(sb_p, cache_sb_kv), kv_sample(sb_s, cache_sb_kv),
            df_p.reshape((depth, bp, sp) + df_shape),
            jnp.swapaxes(df_s.reshape((depth, ss, bs) + df_shape), 1, 2),
            kv_prompt(mb_p, cache_moba_kv), kv_sample(mb_s, cache_moba_kv))
```

```python
import functools
import math

import jax
import jax.numpy as jnp
from jax import lax
from jax.experimental import pallas as pl
from jax.experimental.pallas import tpu as pltpu

F32 = jnp.float32
BF16 = jnp.bfloat16

HEAD_DIM = 64
ROPE_THETA = 10000.0
NORM_EPS = 1e-6
MOBA_BLOCK = 256
MOBA_TOPK = 3
N_BRANCHES = 3

LANES = 128
VMEM_LIMIT_BYTES = 56 * 1024 * 1024

PROMPT_ROW_TILE = 256
ATTN_TILE = 256
MLP_ROW_TILE = 512
MLP_FF_TILE = 1024
DEC_CHUNK = 256

Q_SCALE = math.log2(math.e) * HEAD_DIM ** -0.5
MASKED = -1e30


def _lambda_init(layer):
    return 0.8 - 0.6 * math.exp(-0.3 * layer)


def _params(*semantics):
    return pltpu.CompilerParams(dimension_semantics=semantics, vmem_limit_bytes=VMEM_LIMIT_BYTES)


def _dot(a, b):
    return jnp.dot(a, b, preferred_element_type=F32)


def _dot_nt(a, b, precision=None):
    return lax.dot_general(a, b, (((1,), (1,)), ((), ())), preferred_element_type=F32, precision=precision)


def _rms(x, g):
    return (x * lax.rsqrt(jnp.mean(x * x, axis=-1, keepdims=True) + NORM_EPS)) * g


def _strict_upper(n):
    r = lax.broadcasted_iota(jnp.int32, (n, n), 0)
    c = lax.broadcasted_iota(jnp.int32, (n, n), 1)
    return (r > c).astype(BF16)


def _sb_log_rem(z, past):
    nz = -z
    log_rem = jnp.minimum(nz, 0.0) - jnp.log2(1.0 + jnp.exp2(jnp.minimum(z, nz)))
    return log_rem if past is None else jnp.where(past, log_rem, 0.0)


def _sb_weights(z, log_rem, later, past):
    a = jnp.exp2(z + log_rem + later)
    if past is not None:
        a = jnp.where(past, a, 0.0)
    return a.astype(BF16), later[:, 0:1] + log_rem[:, 0:1]


def _row_max(sc):
    return jnp.max(sc, axis=-1, keepdims=True)


def _rope_tables(pos):
    half = HEAD_DIM // 2
    inv_freq = ROPE_THETA ** (-jnp.arange(half, dtype=F32) / half)
    ang = pos.astype(F32)[:, None] * inv_freq[None, :]
    cos, sin = jnp.cos(ang), jnp.sin(ang)
    reps = LANES // HEAD_DIM
    return (jnp.tile(jnp.concatenate([cos, cos], -1), (1, reps)),
            jnp.tile(jnp.concatenate([-sin, sin], -1), (1, reps)), cos.T, sin.T)


def _rope(v, cos, sin_signed):
    half = HEAD_DIM // 2
    first = (lax.broadcasted_iota(jnp.int32, (v.shape[0], LANES), 1) % HEAD_DIM) < half
    outs = []
    for c in range(v.shape[1] // LANES):
        s = v[:, c * LANES:(c + 1) * LANES]
        swapped = jnp.where(first, pltpu.roll(s, LANES - half, 1), pltpu.roll(s, half, 1))
        outs.append(s * cos + swapped * sin_signed)
    return outs[0] if len(outs) == 1 else jnp.concatenate(outs, axis=-1)


def _rope_feature_major(vt, cos_t, sin_t):
    half = HEAD_DIM // 2
    outs = []
    for h in range(vt.shape[0] // HEAD_DIM):
        x1 = vt[h * HEAD_DIM:h * HEAD_DIM + half]
        x2 = vt[h * HEAD_DIM + half:(h + 1) * HEAD_DIM]
        outs += [x1 * cos_t - x2 * sin_t, x2 * cos_t + x1 * sin_t]
    return jnp.concatenate(outs, axis=0)


def _proj_kernel(x_ref, g_ref, w_ref, wt_ref, cos_ref, sin_ref, cost_ref, sint_ref,
                 qsb_ref, qdf_ref, qmb_ref, kdf_bf_ref, vdf_bf_ref, kvsb_bf_ref, kvmb_bf_ref,
                 kvsb_ref, kvdf_ref, kvmb_ref, kmean_ref,
                 *, sb_w, df_w, mb_w):
    xb = _rms(x_ref[...], g_ref[...]).astype(BF16)
    cos, sin = cos_ref[...], sin_ref[...]

    def mm(lo, width):
        return _dot(xb, w_ref[:, lo:lo + width])

    o = 0
    qsb_ref[...] = (mm(o, sb_w) * Q_SCALE).astype(BF16)
    o += sb_w
    qdf_ref[...] = (_rope(mm(o, df_w), cos, sin) * Q_SCALE).astype(BF16)
    o += df_w
    k_df = _rope(mm(o, df_w), cos, sin)
    o += df_w
    v_df = mm(o, df_w)
    o += df_w
    kdf_bf_ref[...] = k_df.astype(BF16)
    vdf_bf_ref[...] = v_df.astype(BF16)
    groups = df_w // LANES
    for j in range(groups):
        kvdf_ref[:, j, :] = k_df[:, j * LANES:(j + 1) * LANES]
        kvdf_ref[:, groups + j, :] = v_df[:, j * LANES:(j + 1) * LANES]
    qmb_ref[...] = (_rope(mm(o, mb_w), cos, sin) * Q_SCALE).astype(BF16)

    kv_t = _dot_nt(wt_ref[...], xb)
    kv_sb = kv_t[:2 * sb_w]
    kvsb_ref[...] = kv_sb
    kvsb_bf_ref[...] = kv_sb.astype(BF16)
    k_mb = _rope_feature_major(kv_t[2 * sb_w:2 * sb_w + mb_w], cost_ref[...], sint_ref[...])
    v_mb = kv_t[2 * sb_w + mb_w:]
    kvmb_ref[:mb_w] = k_mb
    kvmb_ref[mb_w:] = v_mb
    kvmb_bf_ref[:mb_w] = k_mb.astype(BF16)
    kvmb_bf_ref[mb_w:] = v_mb.astype(BF16)
    kmean_ref[...] = jnp.mean(k_mb, axis=-1, keepdims=True)


def _in_projection(x, g, w_tok, w_feat, layer, tables, widths, tm, seq_tiles):
    t, d = x.shape
    sb_w, df_w, mb_w, _ = widths
    cos, sin, cos_t, sin_t = tables
    tab_tiles = cos.shape[0] // tm
    n_tiles = t // tm
    n_seq = n_tiles // seq_tiles
    s = seq_tiles * tm
    half = HEAD_DIM // 2
    kv_groups = 2 * df_w // LANES

    def rows(w):
        return pl.BlockSpec((tm, w), lambda i: (i, 0))

    def feat(w):
        return pl.BlockSpec((None, w, tm), lambda i: (i // seq_tiles, 0, i % seq_tiles))

    out_shapes = [
        jax.ShapeDtypeStruct((t, sb_w), BF16), jax.ShapeDtypeStruct((t, df_w), BF16),
        jax.ShapeDtypeStruct((t, mb_w), BF16),
        jax.ShapeDtypeStruct((t, df_w), BF16), jax.ShapeDtypeStruct((t, df_w), BF16),
        jax.ShapeDtypeStruct((n_seq, 2 * sb_w, s), BF16), jax.ShapeDtypeStruct((n_seq, 2 * mb_w, s), BF16),
        jax.ShapeDtypeStruct((n_seq, 2 * sb_w, s), F32),
        jax.ShapeDtypeStruct((t, kv_groups, LANES), F32),
        jax.ShapeDtypeStruct((n_seq, 2 * mb_w, s), F32),
        jax.ShapeDtypeStruct((n_tiles, mb_w, 1), F32),
    ]
    out_specs = [rows(sb_w), rows(df_w), rows(mb_w), rows(df_w), rows(df_w), feat(2 * sb_w), feat(2 * mb_w),
                 feat(2 * sb_w), pl.BlockSpec((tm, kv_groups, LANES), lambda i: (i, 0, 0)), feat(2 * mb_w),
                 pl.BlockSpec((None, mb_w, 1), lambda i: (i, 0, 0))]
    return pl.pallas_call(
        functools.partial(_proj_kernel, sb_w=sb_w, df_w=df_w, mb_w=mb_w),
        grid=(n_tiles,),
        in_specs=[rows(d),
                  pl.BlockSpec((None, 1, d), lambda i: (layer, 0, 0)),
                  pl.BlockSpec((None, d, w_tok.shape[-1]), lambda i: (layer, 0, 0)),
                  pl.BlockSpec((None, w_feat.shape[1], d), lambda i: (layer, 0, 0)),
                  pl.BlockSpec((tm, LANES), lambda i: (i % tab_tiles, 0)),
                  pl.BlockSpec((tm, LANES), lambda i: (i % tab_tiles, 0)),
                  pl.BlockSpec((half, tm), lambda i: (0, i % tab_tiles)),
                  pl.BlockSpec((half, tm), lambda i: (0, i % tab_tiles))],
        out_specs=out_specs,
        out_shape=out_shapes,
        compiler_params=_params("parallel"),
        name="in_projection",
    )(x, g, w_tok, w_feat, cos, sin, cos_t, sin_t)


def _causal_pairs(n):
    qi = [i for i in range(n) for _ in range(i + 1)]
    kj = [i - j for i in range(n) for j in range(i + 1)]
    return jnp.asarray(qi, jnp.int32), jnp.asarray(kj, jnp.int32)


def _diag_mask(t, strict):
    r = lax.broadcasted_iota(jnp.int32, (t, t), 0)
    c = lax.broadcasted_iota(jnp.int32, (t, t), 1)
    return (c < r) if strict else (c <= r)


def _sb_prompt_kernel(qi_ref, kj_ref, q_ref, kv_ref, o_ref, acc_ref, carry_ref, *, heads):
    s = pl.program_id(1)
    i, j = qi_ref[s], kj_ref[s]
    t = q_ref.shape[0]

    def sweep(past):
        upper = _strict_upper(t)
        hs = range(heads)
        cols = [slice(h * HEAD_DIM, (h + 1) * HEAD_DIM) for h in hs]
        w = q_ref.shape[1]
        zs = [_dot(q_ref[:, cols[h]], kv_ref[cols[h], :]) for h in hs]
        lrs = [_sb_log_rem(zs[h], past) for h in hs]
        laters = [_dot(lrs[h].astype(BF16), upper) for h in hs]
        weights = [_sb_weights(zs[h], lrs[h], laters[h], past) for h in hs]
        outs = [_dot_nt(weights[h][0], kv_ref[w + h * HEAD_DIM:w + (h + 1) * HEAD_DIM, :]) for h in hs]
        for h in hs:
            carry = carry_ref[h]
            acc_ref[:, cols[h]] += jnp.exp2(carry) * outs[h]
            carry_ref[h] = carry + weights[h][1]

    @pl.when(j == i)
    def _():
        acc_ref[...] = jnp.zeros_like(acc_ref)
        carry_ref[...] = jnp.zeros_like(carry_ref)
        sweep(_diag_mask(t, True))

    @pl.when(j != i)
    def _():
        sweep(None)

    @pl.when(j == 0)
    def _():
        o_ref[...] = acc_ref[...]


def _diff_lambda(lp, lam_init):
    a = jnp.sum(lp[0:1] * lp[1:2], axis=-1, keepdims=True)
    b = jnp.sum(lp[2:3] * lp[3:4], axis=-1, keepdims=True)
    return jnp.exp(a) - jnp.exp(b) + lam_init


def _diff_prompt_kernel(qi_ref, kj_ref, q_ref, k_ref, v_ref, lp_ref, g_ref, o_ref, acc_ref, m_ref,
                        *, heads, lam_init):
    s = pl.program_id(1)
    i, j = qi_ref[s], kj_ref[s]
    t = q_ref.shape[0]
    vw = 2 * HEAD_DIM

    def sweep(causal):
        idxs = [(h, c) for h in range(heads) for c in range(2)]
        scs = []
        for h, c in idxs:
            cols = slice(h * vw + c * HEAD_DIM, h * vw + (c + 1) * HEAD_DIM)
            sc = _dot_nt(q_ref[:, cols], k_ref[:, cols])
            if causal is not None:
                sc = jnp.where(causal, sc, -jnp.inf)
            scs.append(sc)
        alphas, ps = [], []
        for n in range(len(idxs)):
            m_old = m_ref[n]
            m_new = jnp.maximum(m_old, _row_max(scs[n]))
            alphas.append(jnp.exp2(m_old - m_new))
            ps.append(jnp.exp2(scs[n] - jnp.concatenate([m_new] * (t // LANES), axis=1)).astype(BF16))
            m_ref[n] = m_new
        ones = jnp.ones((t, vw), BF16)
        pvs = [_dot(jnp.concatenate(ps[2 * h:2 * h + 2], axis=0),
                    jnp.concatenate([v_ref[:, h * vw:(h + 1) * vw], ones], axis=1)) for h in range(heads)]
        for h in range(heads):
            alpha = jnp.concatenate(alphas[2 * h:2 * h + 2], axis=0)
            acc_ref[h] = jnp.concatenate([alpha] * (2 * vw // LANES), axis=1) * acc_ref[h] + pvs[h]

    @pl.when(j == i)
    def _():
        acc_ref[...] = jnp.zeros_like(acc_ref)
        m_ref[...] = jnp.full_like(m_ref, -jnp.inf)
        sweep(_diag_mask(t, False))

    @pl.when(j != i)
    def _():
        sweep(None)

    @pl.when(j == 0)
    def _():
        lam = _diff_lambda(lp_ref[...], lam_init)
        for h in range(heads):
            o1 = acc_ref[h, 0:t, 0:vw] * (1.0 / acc_ref[h, 0:t, vw:2 * vw])
            o2 = acc_ref[h, t:2 * t, 0:vw] * (1.0 / acc_ref[h, t:2 * t, vw:2 * vw])
            o_ref[:, h * vw:(h + 1) * vw] = _rms(o1 - lam * o2, g_ref[...]) * (1.0 - lam_init)


def _moba_prompt_kernel(qi_ref, kj_ref, q_ref, kv_ref, kmean_ref, o_ref, acc_ref, m_ref, qaug_ref,
                        *, heads, n_blk):
    s = pl.program_id(1)
    i, j = qi_ref[s], kj_ref[s]
    t = q_ref.shape[0]
    gate_rows = ((n_blk + 7) // 8) * 8

    def select_blocks():
        lane = lax.broadcasted_iota(jnp.int32, (t, LANES), 1)
        blk = lax.broadcasted_iota(jnp.int32, (gate_rows, t), 0)
        for h in range(heads):
            qh = q_ref[:, h * HEAD_DIM:(h + 1) * HEAD_DIM].astype(F32)
            gate = _dot_nt(kmean_ref[h], qh, precision=lax.Precision.HIGHEST)[HEAD_DIM:HEAD_DIM + gate_rows]
            rank = jnp.zeros((gate_rows, t), F32)
            for m in range(n_blk):
                gm = gate[m:m + 1, :]
                beats = jnp.logical_or(gm > gate, jnp.logical_and(gm == gate, blk > m))
                rank = rank + jnp.where(beats, 1.0, 0.0) * jnp.where(m < i, 1.0, 0.0)
            attended = jnp.logical_or(jnp.logical_and(rank < MOBA_TOPK, blk < i), blk == i)
            att = jnp.concatenate([jnp.zeros((HEAD_DIM, t), F32), jnp.where(attended, 1.0, 0.0),
                                   jnp.zeros((LANES - HEAD_DIM - gate_rows, t), F32)], axis=0).T
            pair = q_ref[:, (h // 2) * LANES:(h // 2 + 1) * LANES].astype(F32)
            if h % 2:
                pair = pltpu.roll(pair, HEAD_DIM, 1)
            in_gate_lanes = jnp.logical_and(lane >= HEAD_DIM, lane < HEAD_DIM + n_blk)
            pen = jnp.where(jnp.logical_and(in_gate_lanes, att < 0.5), MASKED, 0.0)
            qaug_ref[h] = jnp.where(lane < HEAD_DIM, pair, pen).astype(BF16)

    def sweep(causal):
        rows = lax.broadcasted_iota(jnp.int32, (LANES - HEAD_DIM, t), 0)
        one_hot = jnp.where(rows == j, 1.0, 0.0).astype(BF16)
        ones = jnp.ones((LANES - HEAD_DIM, t), BF16)
        hs = range(heads)
        cols = [slice(h * HEAD_DIM, (h + 1) * HEAD_DIM) for h in hs]
        scs = []
        for h in hs:
            sc = _dot(qaug_ref[h], jnp.concatenate([kv_ref[cols[h], :], one_hot], axis=0))
            if causal is not None:
                sc = jnp.where(causal, sc, -jnp.inf)
            scs.append(sc)
        alphas, ps = [], []
        for h in hs:
            m_old = m_ref[h]
            m_new = jnp.maximum(m_old, _row_max(scs[h]))
            alphas.append(jnp.exp2(m_old - m_new))
            ps.append(jnp.exp2(scs[h] - jnp.concatenate([m_new] * (t // LANES), axis=1)).astype(BF16))
            m_ref[h] = m_new
        w = q_ref.shape[1]
        pvs = [_dot_nt(ps[h], jnp.concatenate([kv_ref[w + h * HEAD_DIM:w + (h + 1) * HEAD_DIM, :], ones], axis=0))
               for h in hs]
        for h in hs:
            acc_ref[h] = alphas[h] * acc_ref[h] + pvs[h]

    @pl.when(j == i)
    def _():
        acc_ref[...] = jnp.zeros_like(acc_ref)
        m_ref[...] = jnp.full_like(m_ref, -jnp.inf)
        select_blocks()
        sweep(_diag_mask(t, False))

    @pl.when(j != i)
    def _():
        sweep(None)

    @pl.when(j == 0)
    def _():
        for h in range(heads):
            o_ref[:, h * HEAD_DIM:(h + 1) * HEAD_DIM] = acc_ref[h, :, 0:HEAD_DIM] / acc_ref[h, :, HEAD_DIM:HEAD_DIM + 1]


def _prompt_branch(kind, q, kv, layer, extra):
    b, s, w = q.shape
    t = ATTN_TILE
    n = s // t
    q_spec = pl.BlockSpec((None, t, w), lambda bb, st, qi, kj: (bb, qi[st], 0))
    kv_spec = pl.BlockSpec((None, 2 * w, t), lambda bb, st, qi, kj: (bb, 0, kj[st]))
    if kind == "sb":
        heads = w // HEAD_DIM
        body = functools.partial(_sb_prompt_kernel, heads=heads)
        in_specs, args = [q_spec, kv_spec], (q, kv)
        scratch = [pltpu.VMEM((t, w), F32), pltpu.VMEM((heads, t, 1), F32)]
    elif kind == "diff":
        heads = w // (2 * HEAD_DIM)
        lp, g = extra
        body = functools.partial(_diff_prompt_kernel, heads=heads, lam_init=_lambda_init(layer))
        tok_spec = pl.BlockSpec((None, t, w), lambda bb, st, qi, kj: (bb, kj[st], 0))
        in_specs = [q_spec, tok_spec, tok_spec,
                    pl.BlockSpec((None, 4, HEAD_DIM), lambda bb, st, qi, kj: (layer, 0, 0)),
                    pl.BlockSpec((None, 1, 2 * HEAD_DIM), lambda bb, st, qi, kj: (layer, 0, 0))]
        args = (q, kv[0], kv[1], lp, g)
        scratch = [pltpu.VMEM((heads, 2 * t, 4 * HEAD_DIM), F32), pltpu.VMEM((2 * heads, t, LANES), F32)]
    else:
        heads = w // HEAD_DIM
        assert HEAD_DIM + n <= LANES
        kmean = extra.reshape(b, n, heads, HEAD_DIM).transpose(0, 2, 1, 3)
        kmean = jnp.pad(kmean, ((0, 0), (0, 0), (HEAD_DIM, LANES - HEAD_DIM - n), (0, 0)))
        body = functools.partial(_moba_prompt_kernel, heads=heads, n_blk=n)
        in_specs = [q_spec, kv_spec,
                    pl.BlockSpec((None, heads, LANES, HEAD_DIM), lambda bb, st, qi, kj: (bb, 0, 0, 0))]
        args = (q, kv, kmean)
        scratch = [pltpu.VMEM((heads, t, LANES), F32), pltpu.VMEM((heads, t, LANES), F32),
                   pltpu.VMEM((heads, t, LANES), BF16)]
    return body, in_specs, args, scratch


def _prompt_attention(branches):
    parts = [_prompt_branch(*br) for br in branches]
    qs = [br[1] for br in branches]
    b, s, _ = qs[0].shape
    t = ATTN_TILE
    qi, kj = _causal_pairs(s // t)
    n_in = [len(p[1]) for p in parts]
    n_scr = [len(p[3]) for p in parts]

    def body(qi_ref, kj_ref, *refs):
        ins, outs, scr = refs[:sum(n_in)], refs[sum(n_in):sum(n_in) + len(parts)], refs[sum(n_in) + len(parts):]
        for k, part in enumerate(parts):
            i0, s0 = sum(n_in[:k]), sum(n_scr[:k])
            part[0](qi_ref, kj_ref, *ins[i0:i0 + n_in[k]], outs[k], *scr[s0:s0 + n_scr[k]])

    return pl.pallas_call(
        body,
        grid_spec=pltpu.PrefetchScalarGridSpec(
            num_scalar_prefetch=2, grid=(b, int(qi.shape[0])),
            in_specs=[sp for p in parts for sp in p[1]],
            out_specs=[pl.BlockSpec((None, t, q.shape[-1]), lambda bb, st, qi, kj: (bb, qi[st], 0)) for q in qs],
            scratch_shapes=[sc for p in parts for sc in p[3]]),
        out_shape=[jax.ShapeDtypeStruct(q.shape, F32) for q in qs],
        compiler_params=_params("parallel", "arbitrary"),
        name="_".join(br[0] for br in branches) + "_prompt_attention",
    )(qi, kj, *[a for p in parts for a in p[2]])


def _block_diag_queries(q, groups, group_w):
    nq, w = q.shape
    rows = lax.broadcasted_iota(jnp.int32, (groups * nq, w), 0) // nq
    cols = lax.broadcasted_iota(jnp.int32, (groups * nq, w), 1) // group_w
    return jnp.where(rows == cols, jnp.concatenate([q] * groups, axis=0), 0.0)


def _head_diagonal(full, heads, nq, head_w):
    cols = lax.broadcasted_iota(jnp.int32, (nq, heads * head_w), 1) // head_w
    out = jnp.zeros((nq, heads * head_w), F32)
    for h in range(heads):
        out = jnp.where(cols == h, full[h * nq:(h + 1) * nq], out)
    return out


def _stage_new_tokens(pad_ref, kv_new):
    pad_ref[...] = jnp.zeros_like(pad_ref)
    pad_ref[0:kv_new.shape[0], :] = kv_new.astype(pad_ref.dtype)


def _feature_major_chunks(pad_ref, page_refs, ppc, w):
    yield True, pad_ref[:, :w].astype(BF16), pad_ref[:, w:].astype(BF16)
    for c in reversed(range(len(page_refs) // ppc)):
        refs = page_refs[c * ppc:(c + 1) * ppc]
        yield (False, jnp.concatenate([r[:w, :] for r in refs], axis=1).astype(BF16),
               jnp.concatenate([r[w:, :] for r in refs], axis=1).astype(BF16))


def _new_token_masks(rows, nq, ck):
    r = lax.broadcasted_iota(jnp.int32, (rows, ck), 0) % nq
    c = lax.broadcasted_iota(jnp.int32, (rows, ck), 1)
    return c < r, c <= r


def _sb_sample_kernel(pt_ref, q_ref, kvn_ref, *rest, heads, n_pages, ppc):
    page_refs, o_ref, pad_ref = rest[:n_pages], rest[n_pages], rest[n_pages + 1]
    nq, w = q_ref.shape[0], q_ref.shape[-1]
    ck = pad_ref.shape[0]
    _stage_new_tokens(pad_ref, kvn_ref[:, 0, :])
    qbd = _block_diag_queries(q_ref[:, 0, :], heads, HEAD_DIM).astype(BF16)
    past_new, _ = _new_token_masks(heads * nq, nq, ck)
    upper = _strict_upper(ck)
    chunks = list(_feature_major_chunks(pad_ref, page_refs, ppc, w))
    masks = [past_new if is_new else None for is_new, _, _ in chunks]
    n = range(len(chunks))
    zs = [_dot_nt(qbd, k) if is_new else _dot(qbd, k) for is_new, k, _ in chunks]
    lrs = [_sb_log_rem(zs[c], masks[c]) for c in n]
    laters = [_dot(lrs[c].astype(BF16), upper) for c in n]
    weights = [_sb_weights(zs[c], lrs[c], laters[c], masks[c]) for c in n]
    outs = [_dot(weights[c][0], chunks[c][2]) if chunks[c][0] else _dot_nt(weights[c][0], chunks[c][2]) for c in n]
    acc = jnp.zeros((heads * nq, w), F32)
    carry = jnp.zeros((heads * nq, 1), F32)
    for c in n:
        acc = acc + jnp.exp2(carry) * outs[c]
        carry = carry + weights[c][1]
    o_ref[:, 0, :] = _head_diagonal(acc, heads, nq, HEAD_DIM)


def _softmax_over_chunks(scs, values, token_major):
    m = functools.reduce(jnp.maximum, [_row_max(sc) for sc in scs])
    ps = [jnp.exp2(sc - m) for sc in scs]
    l = sum(jnp.sum(p, axis=-1, keepdims=True) for p in ps)
    outs = [_dot(p.astype(BF16), v) if tm else _dot_nt(p.astype(BF16), v) for p, v, tm in zip(ps, values, token_major)]
    return sum(outs) / l


def _moba_sample_kernel(pt_ref, q_ref, kvn_ref, *rest, heads, n_pages, ppc):
    page_refs, o_ref, pad_ref = rest[:n_pages], rest[n_pages], rest[n_pages + 1]
    nq, w = q_ref.shape[0], q_ref.shape[-1]
    ck = pad_ref.shape[0]
    _stage_new_tokens(pad_ref, kvn_ref[:, 0, :])
    qf = _block_diag_queries(q_ref[:, 0, :].astype(F32), heads, HEAD_DIM)
    qbd = qf.astype(BF16)
    rows = heads * nq
    _, causal_new = _new_token_masks(rows, nq, ck)
    n_blk = n_pages // ppc
    lane = lax.broadcasted_iota(jnp.int32, (w, LANES), 1)
    kmean = jnp.zeros((w, LANES), F32)
    for b in range(n_blk):
        k_sum = sum(jnp.sum(page_refs[b * ppc + p][:w, :], axis=-1, keepdims=True) for p in range(ppc))
        kmean = kmean + jnp.where(lane == b, k_sum * (1.0 / ck), 0.0)
    gate = jnp.dot(qf, kmean, preferred_element_type=F32, precision=lax.Precision.HIGHEST)
    blk_lane = lax.broadcasted_iota(jnp.int32, (rows, LANES), 1)
    rank = jnp.zeros((rows, LANES), F32)
    for m in range(n_blk):
        gm = gate[:, m:m + 1]
        beats = jnp.logical_or(gm > gate, jnp.logical_and(gm == gate, blk_lane > m))
        rank = rank + jnp.where(beats, 1.0, 0.0)
    pen = jnp.where(rank < MOBA_TOPK, 0.0, MASKED)
    chunks = list(_feature_major_chunks(pad_ref, page_refs, ppc, w))
    scs = []
    for c, (is_new, k, _) in enumerate(chunks):
        if is_new:
            scs.append(jnp.where(causal_new, _dot_nt(qbd, k), -jnp.inf))
        else:
            blk = n_blk - c
            scs.append(_dot(qbd, k) + pen[:, blk:blk + 1])
    o = _softmax_over_chunks(scs, [v for _, _, v in chunks], [is_new for is_new, _, _ in chunks])
    o_ref[:, 0, :] = _head_diagonal(o, heads, nq, HEAD_DIM)


def _diff_sample_kernel(pt_ref, q_ref, kn_ref, vn_ref, lp_ref, g_ref, *rest, heads, n_pages, ppc, lam_init):
    page_refs, o_ref, pad_ref = rest[:n_pages], rest[n_pages], rest[n_pages + 1]
    nq, w = q_ref.shape[0], q_ref.shape[-1]
    ck = pad_ref.shape[0]
    vw = 2 * HEAD_DIM
    groups = 2 * heads
    page = page_refs[0].shape[0] // groups

    def token_major(ref, first):
        return jnp.concatenate([ref[pl.ds(first + h, page, stride=groups), :] for h in range(heads)], axis=-1)

    _stage_new_tokens(pad_ref, jnp.concatenate([kn_ref[:, 0, :], vn_ref[:, 0, :]], axis=-1))
    qg = _block_diag_queries(q_ref[:, 0, :].astype(F32), groups, HEAD_DIM)
    qbd = jnp.concatenate([qg[(2 * h + c) * nq:(2 * h + c + 1) * nq] for c in range(2) for h in range(heads)],
                          axis=0).astype(BF16)
    rows = groups * nq
    _, causal_new = _new_token_masks(rows, nq, ck)
    scs = [jnp.where(causal_new, _dot_nt(qbd, pad_ref[:, :w].astype(BF16)), -jnp.inf)]
    values = [pad_ref[:, w:].astype(BF16)]
    for c in reversed(range(n_pages // ppc)):
        refs = page_refs[c * ppc:(c + 1) * ppc]
        k = jnp.concatenate([token_major(r, 0) for r in refs], axis=0).astype(BF16)
        scs.append(_dot_nt(qbd, k))
        values.append(jnp.concatenate([token_major(r, heads) for r in refs], axis=0).astype(BF16))
    o = _softmax_over_chunks(scs, values, [True] * len(scs))
    half = heads * nq
    lam = _diff_lambda(lp_ref[...], lam_init)
    od = _head_diagonal(o[:half] - lam * o[half:], heads, nq, vw)
    for h in range(heads):
        o_ref[:, 0, h * vw:(h + 1) * vw] = _rms(od[:, h * vw:(h + 1) * vw], g_ref[...]) * (1.0 - lam_init)


def _sample_attention(kind, q, kv_new, cache, page_table_flat, n_pages, layer, extra):
    nq, db, _, w = q.shape
    page = cache.shape[3] if kind != "diff" else cache.shape[2] * cache.shape[3] // (2 * w)
    ppc = DEC_CHUNK // page
    assert ppc * page == DEC_CHUNK and n_pages % ppc == 0 and nq <= DEC_CHUNK
    tok = lambda a: pl.BlockSpec((nq, None) + a.shape[2:], lambda b, pt: (0, b, 0, 0))
    page_specs = [pl.BlockSpec((None, None) + cache.shape[2:], lambda b, pt, p=p: (layer, pt[b * n_pages + p], 0, 0))
                  for p in range(n_pages)]
    lead_specs, lead_args = [tok(q)] + [tok(a) for a in kv_new], (q,) + tuple(kv_new)
    if kind == "sb":
        body = functools.partial(_sb_sample_kernel, heads=w // HEAD_DIM, n_pages=n_pages, ppc=ppc)
    elif kind == "moba":
        body = functools.partial(_moba_sample_kernel, heads=w // HEAD_DIM, n_pages=n_pages, ppc=ppc)
    else:
        lp, g = extra
        body = functools.partial(_diff_sample_kernel, heads=w // (2 * HEAD_DIM), n_pages=n_pages, ppc=ppc,
                                 lam_init=_lambda_init(layer))
        lead_specs += [pl.BlockSpec((None, 4, HEAD_DIM), lambda b, pt: (layer, 0, 0)),
                       pl.BlockSpec((None, 1, 2 * HEAD_DIM), lambda b, pt: (layer, 0, 0))]
        lead_args += (lp, g)
    return pl.pallas_call(
        body,
        grid_spec=pltpu.PrefetchScalarGridSpec(
            num_scalar_prefetch=1, grid=(db,),
            in_specs=lead_specs + page_specs, out_specs=tok(jax.ShapeDtypeStruct(q.shape, F32)),
            scratch_shapes=[pltpu.VMEM((DEC_CHUNK, 2 * w), F32)]),
        out_shape=jax.ShapeDtypeStruct(q.shape, F32),
        compiler_params=_params("parallel"),
        name=kind + "_sample_attention",
    )(page_table_flat, *lead_args, *([cache] * n_pages))


def _merge_kernel(x_ref, g_ref, osb_ref, odf_ref, omb_ref, wg_ref, b_ref, wsb_ref, wdf_ref, wmb_ref, wout_ref, o_ref):
    d = x_ref.shape[1]
    x = x_ref[...]
    xb = _rms(x, g_ref[...]).astype(BF16)
    merged = jnp.zeros(x.shape, F32)
    for n, (o_r, w_r) in enumerate(((osb_ref, wsb_ref), (odf_ref, wdf_ref), (omb_ref, wmb_ref))):
        y = _dot(o_r[...].astype(BF16), w_r[...])
        gate = jax.nn.sigmoid(_dot(xb, wg_ref[:, n * d:(n + 1) * d]) + b_ref[n:n + 1, :])
        merged = merged + gate * y
    o_ref[...] = x + _dot(merged.astype(BF16), wout_ref[...])


def _merge(x, g_mix, o_sb, o_df, o_mb, w_gate, b_gate, w_sb, w_df, w_mb, w_out, layer, tm):
    t, d = x.shape
    rows = lambda w: pl.BlockSpec((tm, w), lambda i: (i, 0))
    lay = lambda a: pl.BlockSpec((None,) + a.shape[1:], lambda i: (layer, 0, 0))
    return pl.pallas_call(
        _merge_kernel,
        grid=(t // tm,),
        in_specs=[rows(d), lay(g_mix), rows(o_sb.shape[1]), rows(o_df.shape[1]), rows(o_mb.shape[1]),
                  lay(w_gate), lay(b_gate), lay(w_sb), lay(w_df), lay(w_mb), lay(w_out)],
        out_specs=rows(d),
        out_shape=jax.ShapeDtypeStruct((t, d), F32),
        compiler_params=_params("parallel"),
        name="merge",
    )(x, g_mix, o_sb, o_df, o_mb, w_gate, b_gate, w_sb, w_df, w_mb, w_out)


def _mlp_kernel(x_ref, g_ref, wup_ref, wdn_ref, gf_ref, o_ref, hn_ref, acc_ref, *, final_norm):
    f = pl.program_id(1)

    @pl.when(f == 0)
    def _():
        hn_ref[...] = _rms(x_ref[...], g_ref[...]).astype(BF16)
        acc_ref[...] = x_ref[...]

    up = _dot(hn_ref[...], wup_ref[...])
    act = jnp.square(jnp.maximum(up, 0.0))
    acc_ref[...] += _dot(act.astype(BF16), wdn_ref[...])

    @pl.when(f == pl.num_programs(1) - 1)
    def _():
        y = acc_ref[...]
        o_ref[...] = _rms(y, gf_ref[...]) if final_norm else y


def _mlp(x, g_mlp, w_up, w_down, g_final, layer, final_norm):
    t, d = x.shape
    ff = w_up.shape[-1]
    tm = min(MLP_ROW_TILE, t)
    tf = MLP_FF_TILE
    return pl.pallas_call(
        functools.partial(_mlp_kernel, final_norm=final_norm),
        grid=(t // tm, ff // tf),
        in_specs=[pl.BlockSpec((tm, d), lambda i, f: (i, 0)),
                  pl.BlockSpec((None, 1, d), lambda i, f: (layer, 0, 0)),
                  pl.BlockSpec((None, d, tf), lambda i, f: (layer, 0, f)),
                  pl.BlockSpec((None, tf, d), lambda i, f: (layer, f, 0)),
                  pl.BlockSpec((1, d), lambda i, f: (0, 0))],
        out_specs=pl.BlockSpec((tm, d), lambda i, f: (i, 0)),
        out_shape=jax.ShapeDtypeStruct((t, d), F32),
        scratch_shapes=[pltpu.VMEM((tm, d), BF16), pltpu.VMEM((tm, d), F32)],
        compiler_params=_params("parallel", "arbitrary"),
        name="mlp",
    )(x, g_mlp, w_up, w_down, g_final)


def _split_w_in(w_in, widths):
    sb_w, df_w, mb_w, gate_w = widths
    o_df = 3 * sb_w
    o_mb = o_df + 3 * df_w
    o_g = o_mb + 3 * mb_w
    w_tok = jnp.concatenate([w_in[..., :sb_w], w_in[..., o_df:o_mb + mb_w]], axis=-1)
    w_feat = jnp.concatenate([w_in[..., sb_w:o_df], w_in[..., o_mb + mb_w:o_g]], axis=-1)
    return w_tok.astype(BF16), jnp.swapaxes(w_feat, 1, 2).astype(BF16), w_in[..., o_g:o_g + gate_w].astype(BF16)


def _trunk(x, tile_pos, tm, seq_tiles, caches, page_table, weights, widths):
    (g_mix, w_tok, w_feat, w_gate, b_gate, diff_lambda, diff_subln_g, w_br_sb, w_br_diff, w_br_moba, w_out,
     g_mlp, w_up, w_down, g_final) = weights
    t, d = x.shape
    depth = w_tok.shape[0]
    sb_w, df_w, mb_w, _ = widths
    n_seq = t // (tm * seq_tiles)
    s = tm * seq_tiles
    assert t % tm == 0 and t % min(MLP_ROW_TILE, t) == 0 and w_up.shape[-1] % MLP_FF_TILE == 0
    tables = _rope_tables(tile_pos)
    if caches is not None:
        n_pages = page_table.shape[1]
        pt_flat = page_table.reshape(-1)
    new_sb, new_df, new_mb = [], [], []
    for layer in range(depth):
        (q_sb, q_df, q_mb, k_df_bf, v_df_bf, kv_sb_bf, kv_mb_bf, kv_sb, kv_df, kv_mb, kmean) = _in_projection(
            x, g_mix, w_tok, w_feat, layer, tables, widths, tm, seq_tiles)
        if caches is None:
            r3 = lambda a: a.reshape(n_seq, s, a.shape[-1])
            o_sb, o_df, o_mb = _prompt_attention([
                ("sb", r3(q_sb), kv_sb_bf, layer, None),
                ("diff", r3(q_df), (r3(k_df_bf), r3(v_df_bf)), layer, (diff_lambda, diff_subln_g)),
                ("moba", r3(q_mb), kv_mb_bf, layer, kmean.reshape(n_seq, s // MOBA_BLOCK, mb_w))])
        else:
            r4 = lambda a: a.astype(F32).reshape(n_seq, tm, 1, a.shape[-1])
            tok_major = lambda kv_t: jnp.swapaxes(kv_t, 1, 2).astype(F32).reshape(n_seq, tm, 1, kv_t.shape[1])
            o_sb = _sample_attention("sb", r4(q_sb), (tok_major(kv_sb_bf),), caches[0], pt_flat, n_pages, layer, None)
            o_df = _sample_attention("diff", r4(q_df), (r4(k_df_bf), r4(v_df_bf)), caches[1], pt_flat, n_pages, layer,
                                     (diff_lambda, diff_subln_g))
            o_mb = _sample_attention("moba", r4(q_mb), (tok_major(kv_mb_bf),), caches[2], pt_flat, n_pages, layer, None)
        x = _merge(x, g_mix, o_sb.reshape(t, sb_w), o_df.reshape(t, df_w), o_mb.reshape(t, mb_w), w_gate,
                   b_gate, w_br_sb, w_br_diff, w_br_moba, w_out, layer, tm)
        x = _mlp(x, g_mlp, w_up, w_down, g_final, layer, layer == depth - 1)
        new_sb.append(kv_sb)
        new_df.append(kv_df)
        new_mb.append(kv_mb)
    return x, jnp.stack(new_sb), jnp.stack(new_df), jnp.stack(new_mb)


def kernel(x_prompt, x_sample, cache_sb_kv, cache_diff_kv, cache_moba_kv, page_table, g_mix, w_in, b_gate, diff_lambda, diff_subln_g, w_br_sb, w_br_diff, w_br_moba, w_out, g_mlp, w_up, w_down, g_final):
    depth, d = w_in.shape[0], w_in.shape[1]
    bp, sp = x_prompt.shape[:2]
    bs, ss = x_sample.shape[:2]
    page = cache_sb_kv.shape[2]
    past_len = page_table.shape[1] * page
    widths = (w_br_sb.shape[1], w_br_diff.shape[1], w_br_moba.shape[1], N_BRANCHES * d)
    assert sp % ATTN_TILE == 0 and ATTN_TILE == MOBA_BLOCK == PROMPT_ROW_TILE
    assert past_len % MOBA_BLOCK == 0 and ss <= MOBA_BLOCK and DEC_CHUNK == MOBA_BLOCK and bs % 8 == 0

    bf = lambda a: a.astype(BF16)
    row = lambda a: a.reshape(depth, 1, a.shape[-1])
    w_tok, w_feat, w_gate = _split_w_in(w_in, widths)
    weights = (row(g_mix), w_tok, w_feat, w_gate, b_gate, diff_lambda, row(diff_subln_g), bf(w_br_sb), bf(w_br_diff),
               bf(w_br_moba), bf(w_out), row(g_mlp), bf(w_up), bf(w_down), g_final.reshape(1, -1))

    feat_major = lambda c: jnp.transpose(c, (0, 1, 3, 4, 5, 2)).reshape(c.shape[:2] + (-1, c.shape[2]))
    caches = (feat_major(cache_sb_kv), cache_diff_kv.reshape(cache_diff_kv.shape[:2] + (-1, cache_diff_kv.shape[-1])),
              feat_major(cache_moba_kv))

    pos_p = jnp.arange(sp, dtype=jnp.int32)
    y_p, sb_p, df_p, mb_p = _trunk(x_prompt.reshape(bp * sp, d), pos_p, PROMPT_ROW_TILE, sp // PROMPT_ROW_TILE,
                                   None, None, weights, widths)
    pos_s = jnp.repeat(past_len + jnp.arange(ss, dtype=jnp.int32), bs)
    x_s = jnp.swapaxes(x_sample, 0, 1).reshape(ss * bs, d)
    y_s, sb_s, df_s, mb_s = _trunk(x_s, pos_s, bs, 1, caches, page_table, weights, widths)

    def kv_prompt(kv_t, ref_cache):
        a = kv_t.reshape((depth, bp) + ref_cache.shape[3:] + (sp,))
        return jnp.transpose(a, (0, 1, 5, 2, 3, 4))

    def kv_sample(kv_t, ref_cache):
        a = kv_t.reshape((depth, ss) + ref_cache.shape[3:] + (bs,))
        return jnp.transpose(a, (0, 5, 1, 2, 3, 4))

    df_shape = cache_diff_kv.shape[3:]
    return (y_p.reshape(bp, sp, d), jnp.swapaxes(y_s.reshape(ss, bs, d), 0, 1),
            kv_prompt(sb_p, cache_sb_kv), kv_sample(sb_s, cache_sb_kv),
            df_p.reshape((depth, bp, sp) + df_shape),
            jnp.swapaxes(df_s.reshape((depth, ss, bs) + df_shape), 1, 2),
            kv_prompt(mb_p, cache_moba_kv), kv_sample(mb_s, cache_moba_kv))
```
